```python
import jax
import jax.numpy as jnp
from jax import lax
import numpy as np

D_MODEL = 1024
BATCH = 8
SEQ = 2048
DEPTH = 4
DEC_BATCH = 32
DEC_SEQ = 4
PAST_LEN = 8192
PAGE_SIZE = 128

N_MIXERS = 4
D_BRANCH = D_MODEL // N_MIXERS
HEAD_DIM = 64
N_HEADS_A = D_BRANCH // HEAD_DIM
N_HEADS_B = D_BRANCH // HEAD_DIM
LORA_W = 64
LORA_A = 64
B_SHIFT_W = 3 * D_BRANCH + LORA_W + LORA_A
RWKV_SPLITS = (D_BRANCH, 2 * D_BRANCH, 3 * D_BRANCH, 3 * D_BRANCH + LORA_W)
POOL_WINDOWS = (2, 4, 8, 16)
N_POOL_GROUPS = len(POOL_WINDOWS)
POOL_GROUP_DIM = D_BRANCH // N_POOL_GROUPS
POOL_BUF = max(POOL_WINDOWS) - 1
CONV_WIDTH = 31
CONV_BUF = CONV_WIDTH - 1
Q_BLOCK = 128
IN_SIZES = (D_BRANCH, D_BRANCH, D_BRANCH, N_HEADS_A, D_BRANCH,
            B_SHIFT_W, D_BRANCH,
            D_BRANCH, D_BRANCH,
            2 * D_BRANCH, D_BRANCH)
D_IN = sum(IN_SIZES)
D_MIX = N_MIXERS * D_BRANCH
RMS_EPS = 1e-6
LN_EPS = 1e-5
GN_EPS = 64e-5
ATTN_SCALE = HEAD_DIM ** -0.5

kernel_name = 'hymba_fox_rwkv7_pool_conformer_step'


def rmsnorm(x, g):
    x32 = x.astype(jnp.float32)
    y = x32 * lax.rsqrt(jnp.mean(x32 * x32, axis=-1, keepdims=True) + RMS_EPS)
    return (y * g.astype(jnp.float32)).astype(x.dtype)


def split_cols(z):
    bounds = []
    acc = 0
    for s in IN_SIZES[:-1]:
        acc += s
        bounds.append(acc)
    return jnp.split(z, bounds, axis=-1)


def heads(t, n_heads):
    return t.reshape(t.shape[0], t.shape[1], n_heads, HEAD_DIM)


def fox_block(q, k, v, c_q, c_k, q_pos, k_pos):
    s = jnp.einsum('bqhd,bkhd->bhqk', q, k, preferred_element_type=jnp.float32) * ATTN_SCALE
    s = s + jnp.swapaxes(c_q, 1, 2)[:, :, :, None] - jnp.swapaxes(c_k, 1, 2)[:, :, None, :]
    s = jnp.where(k_pos[None, :] <= q_pos[:, None], s, -jnp.inf)
    p = jax.nn.softmax(s, axis=-1)
    return jnp.einsum('bhqk,bkhd->bqhd', p.astype(v.dtype), v)


def fox_prompt(q, k, v, logf):
    t = q.shape[1]
    c = jnp.cumsum(logf, axis=1)
    pos = jnp.arange(t)
    outs = []
    for i in range(t // Q_BLOCK):
        lo, hi = i * Q_BLOCK, (i + 1) * Q_BLOCK
        outs.append(fox_block(q[:, lo:hi], k[:, :hi], v[:, :hi], c[:, lo:hi], c[:, :hi], pos[lo:hi], pos[:hi]))
    return jnp.concatenate(outs, axis=1)


def fox_sample(q, k, v, logf, cache_k, cache_v, cache_logf, page_table, layer):
    db, t = q.shape[0], q.shape[1]
    past = page_table.shape[1] * PAGE_SIZE
    k_past = cache_k[page_table, layer].reshape(db, past, N_HEADS_A, HEAD_DIM)
    v_past = cache_v[page_table, layer].reshape(db, past, N_HEADS_A, HEAD_DIM)
    lf_past = cache_logf[page_table, layer].reshape(db, past, N_HEADS_A).astype(jnp.float32)
    k_all = jnp.concatenate([k_past.astype(k.dtype), k], axis=1)
    v_all = jnp.concatenate([v_past.astype(v.dtype), v], axis=1)
    c = jnp.cumsum(jnp.concatenate([lf_past, logf], axis=1), axis=1)
    q_pos = past + jnp.arange(t)
    k_pos = jnp.arange(past + t)
    return fox_block(q, k_all, v_all, c[:, past:], c, q_pos, k_pos)


def rwkv7_step(s, inp):
    r_t, w_t, k_t, v_t, kk_t, a_t = inp
    s_kk = jnp.einsum('bhvk,bhk->bhv', s, kk_t)
    s = s * w_t[:, :, None, :] - s_kk[..., None] * (kk_t * a_t)[:, :, None, :] + v_t[..., None] * k_t[:, :, None, :]
    o = jnp.einsum('bhvk,bhk->bhv', s, r_t)
    return s, o


def rwkv7_mix(sh, shift_prev, s0, rw_mu, rw_w0, rw_wup, rw_a0, rw_aup, rw_kk, rw_ka, rw_rk, rw_gn_g, rw_gn_b):
    b, t, _ = sh.shape
    f32 = jnp.float32
    prev = jnp.concatenate([shift_prev[:, None, :].astype(sh.dtype), sh[:, :-1]], axis=1)
    xm = sh + rw_mu * (prev - sh)
    r, k, v, wd, ad = jnp.split(xm.astype(f32), RWKV_SPLITS, axis=-1)
    w = -jax.nn.softplus(-(rw_w0 + jnp.tanh(wd) @ rw_wup.astype(f32))) - 0.5
    decay = jnp.exp(-jnp.exp(w))
    a = jax.nn.sigmoid(rw_a0 + ad @ rw_aup.astype(f32))
    kk = heads(k * rw_kk, N_HEADS_B)
    kk = kk * lax.rsqrt(jnp.maximum(jnp.sum(kk * kk, axis=-1, keepdims=True), 1e-12))
    k = k * (1.0 + (a - 1.0) * rw_ka)
    rh, dh, kh, vh, ah = (heads(z, N_HEADS_B) for z in (r, decay, k, v, a))
    xs = tuple(jnp.moveaxis(z, 1, 0) for z in (rh, dh, kh, vh, kk, ah))
    s_final, o = lax.scan(rwkv7_step, s0.astype(f32), xs)
    o = jnp.moveaxis(o, 0, 1)
    mean = jnp.mean(o, axis=-1, keepdims=True)
    var = jnp.mean(jnp.square(o - mean), axis=-1, keepdims=True)
    o = ((o - mean) * lax.rsqrt(var + GN_EPS)).reshape(b, t, D_BRANCH) * rw_gn_g + rw_gn_b
    bonus = jnp.sum(rh * kh * rw_rk, axis=-1, keepdims=True) * vh
    o = o + bonus.reshape(b, t, D_BRANCH)
    return o.astype(sh.dtype), s_final, sh[:, -1]


def pool_mixer(u, buf, pos, pool_w, pool_scale):
    b, t, _ = u.shape
    ext = jnp.concatenate([buf.astype(u.dtype), u], axis=1)
    cs = jnp.cumsum(ext.astype(jnp.float32), axis=1)
    cs = jnp.concatenate([jnp.zeros_like(cs[:, :1]), cs], axis=1)
    groups = []
    for g, w in enumerate(POOL_WINDOWS):
        lo, hi = g * POOL_GROUP_DIM, (g + 1) * POOL_GROUP_DIM
        win = cs[:, POOL_BUF + 1:POOL_BUF + 1 + t, lo:hi] - cs[:, POOL_BUF + 1 - w:POOL_BUF + 1 - w + t, lo:hi]
        cnt = jnp.minimum(pos + 1, w).astype(jnp.float32)
        groups.append(win / cnt[None, :, None])
    pooled = jnp.concatenate(groups, axis=-1) - u.astype(jnp.float32)
    y = jnp.einsum('btgc,gcd->btgd', pooled.reshape(b, t, N_POOL_GROUPS, POOL_GROUP_DIM), pool_w.astype(jnp.float32))
    y = y.reshape(b, t, D_BRANCH) * pool_scale
    return y.astype(u.dtype), ext[:, -POOL_BUF:]


def conv_module(u, buf, conv_w, conv_b, ln_g, ln_b, pw_out):
    u1, u2 = jnp.split(u, 2, axis=-1)
    glu = u1 * jax.nn.sigmoid(u2)
    ext = jnp.concatenate([buf.astype(glu.dtype), glu], axis=1)
    y = lax.conv_general_dilated(ext, conv_w[:, None, :].astype(ext.dtype), window_strides=(1,), padding='VALID',
                                 dimension_numbers=('NWC', 'WIO', 'NWC'), feature_group_count=D_BRANCH) + conv_b
    y32 = y.astype(jnp.float32)
    mean = jnp.mean(y32, axis=-1, keepdims=True)
    var = jnp.mean(jnp.square(y32 - mean), axis=-1, keepdims=True)
    yn = (y32 - mean) * lax.rsqrt(var + LN_EPS) * ln_g + ln_b
    out = jax.nn.silu(yn).astype(u.dtype) @ pw_out
    return out, ext[:, -CONV_BUF:]


def mixer_layer(x, pos, attend, wkv0, shift0, pool0, conv0,
                norm_g, w_in, fox_bf, rw_mu, rw_w0, rw_wup, rw_a0, rw_aup, rw_kk, rw_ka, rw_rk, rw_gn_g, rw_gn_b,
                pool_w, pool_scale, conv_w, conv_b, ln_g, ln_b, pw_out, w_out):
    b, t, _ = x.shape
    h = rmsnorm(x, norm_g)
    z = jnp.einsum('btd,de->bte', h, w_in)
    q_a, k_att, v_a, f_a, g_a, sh_b, g_b, u_c, g_c, u_d, g_d = split_cols(z)
    logf = jax.nn.log_sigmoid((f_a + fox_bf).astype(jnp.float32))
    qh, kh, vh = heads(q_a, N_HEADS_A), heads(k_att, N_HEADS_A), heads(v_a, N_HEADS_A)
    o_a = attend(qh, kh, vh, logf).reshape(b, t, D_BRANCH)
    o_b, wkv1, shift1 = rwkv7_mix(sh_b, shift0, wkv0, rw_mu, rw_w0, rw_wup, rw_a0, rw_aup, rw_kk, rw_ka, rw_rk, rw_gn_g, rw_gn_b)
    o_c, pool1 = pool_mixer(u_c, pool0, pos, pool_w, pool_scale)
    o_d, conv1 = conv_module(u_d, conv0, conv_w, conv_b, ln_g, ln_b, pw_out)
    mixed = jnp.concatenate([o_a * jax.nn.silu(g_a), o_b * jax.nn.silu(g_b),
                             o_c * jax.nn.silu(g_c), o_d * jax.nn.silu(g_d)], axis=-1)
    x = x + jnp.einsum('bte,ed->btd', mixed, w_out)
    return x, kh, vh, logf, wkv1, shift1, pool1, conv1


def setup_inputs(seed: int = 0) -> dict:
    key = jax.random.key(seed)
    ks = jax.random.split(key, 40)
    f32 = jnp.float32
    n_pages = PAST_LEN // PAGE_SIZE
    n_used = DEC_BATCH * n_pages
    n_phys = n_used + n_used // 4

    def nrm(k, shape, s):
        return s * jax.random.normal(k, shape, f32)

    page_table = jax.random.permutation(ks[0], n_phys)[:n_used].reshape(DEC_BATCH, n_pages).astype(jnp.int32)
    return {
        'x_prompt': nrm(ks[1], (BATCH, SEQ, D_MODEL), 1.0),
        'x_sample': nrm(ks[2], (DEC_BATCH, DEC_SEQ, D_MODEL), 1.0),
        'cache_k': nrm(ks[3], (n_phys, DEPTH, PAGE_SIZE, N_HEADS_A, HEAD_DIM), 1.0),
        'cache_v': nrm(ks[4], (n_phys, DEPTH, PAGE_SIZE, N_HEADS_A, HEAD_DIM), 1.0),
        'cache_logf': jax.nn.log_sigmoid(3.0 + jax.random.normal(ks[5], (n_phys, DEPTH, PAGE_SIZE, N_HEADS_A), f32)),
        'state_wkv': nrm(ks[6], (DEC_BATCH, DEPTH, N_HEADS_B, HEAD_DIM, HEAD_DIM), 0.1),
        'state_shift': nrm(ks[7], (DEC_BATCH, DEPTH, B_SHIFT_W), 1.0),
        'state_pool': nrm(ks[8], (DEC_BATCH, DEPTH, POOL_BUF, D_BRANCH), 1.0),
        'state_conv': nrm(ks[9], (DEC_BATCH, DEPTH, CONV_BUF, D_BRANCH), 0.5),
        'page_table': page_table,
        'norm_g': 1.0 + nrm(ks[10], (DEPTH, D_MODEL), 0.05),
        'w_in': nrm(ks[11], (DEPTH, D_MODEL, D_IN), D_MODEL ** -0.5),
        'fox_bf': jax.random.uniform(ks[12], (DEPTH, N_HEADS_A), f32, 1.0, 5.0),
        'rw_mu': jax.random.uniform(ks[13], (DEPTH, B_SHIFT_W), f32, 0.0, 1.0),
        'rw_w0': nrm(ks[14], (DEPTH, D_BRANCH), 0.5) - 0.5,
        'rw_wup': nrm(ks[15], (DEPTH, LORA_W, D_BRANCH), 0.5 * LORA_W ** -0.5),
        'rw_a0': nrm(ks[16], (DEPTH, D_BRANCH), 0.1),
        'rw_aup': nrm(ks[17], (DEPTH, LORA_A, D_BRANCH), 0.5 * LORA_A ** -0.5),
        'rw_kk': 0.85 + nrm(ks[18], (DEPTH, D_BRANCH), 0.05),
        'rw_ka': 1.0 + nrm(ks[19], (DEPTH, D_BRANCH), 0.05),
        'rw_rk': nrm(ks[20], (DEPTH, N_HEADS_B, HEAD_DIM), 0.1),
        'rw_gn_g': 1.0 + nrm(ks[21], (DEPTH, D_BRANCH), 0.05),
        'rw_gn_b': nrm(ks[22], (DEPTH, D_BRANCH), 0.02),
        'pool_w': nrm(ks[23], (DEPTH, N_POOL_GROUPS, POOL_GROUP_DIM, POOL_GROUP_DIM), POOL_GROUP_DIM ** -0.5),
        'pool_scale': 1.0 + nrm(ks[24], (DEPTH, D_BRANCH), 0.05),
        'conv_w': nrm(ks[25], (DEPTH, CONV_WIDTH, D_BRANCH), CONV_WIDTH ** -0.5),
        'conv_b': nrm(ks[26], (DEPTH, D_BRANCH), 0.02),
        'ln_g': 1.0 + nrm(ks[27], (DEPTH, D_BRANCH), 0.05),
        'ln_b': nrm(ks[28], (DEPTH, D_BRANCH), 0.02),
        'pw_out': nrm(ks[29], (DEPTH, D_BRANCH, D_BRANCH), D_BRANCH ** -0.5),
        'w_out': nrm(ks[30], (DEPTH, D_MIX, D_MODEL), D_MIX ** -0.5),
        'final_norm_g': 1.0 + nrm(ks[31], (D_MODEL,), 0.05),
    }


def reference(x_prompt, x_sample, cache_k, cache_v, cache_logf, state_wkv, state_shift, state_pool, state_conv,
              page_table, norm_g, w_in, fox_bf, rw_mu, rw_w0, rw_wup, rw_a0, rw_aup, rw_kk, rw_ka, rw_rk,
              rw_gn_g, rw_gn_b, pool_w, pool_scale, conv_w, conv_b, ln_g, ln_b, pw_out, w_out, final_norm_g):
    weights = (norm_g, w_in, fox_bf, rw_mu, rw_w0, rw_wup, rw_a0, rw_aup, rw_kk, rw_ka, rw_rk, rw_gn_g, rw_gn_b,
               pool_w, pool_scale, conv_w, conv_b, ln_g, ln_b, pw_out, w_out)
    b_p, t_p = x_prompt.shape[0], x_prompt.shape[1]
    past = page_table.shape[1] * PAGE_SIZE
    pos_p = jnp.arange(t_p)
    pos_s = past + jnp.arange(x_sample.shape[1])
    wkv_zero = jnp.zeros((b_p, N_HEADS_B, HEAD_DIM, HEAD_DIM), jnp.float32)
    shift_zero = jnp.zeros((b_p, B_SHIFT_W), x_prompt.dtype)
    pool_zero = jnp.zeros((b_p, POOL_BUF, D_BRANCH), x_prompt.dtype)
    conv_zero = jnp.zeros((b_p, CONV_BUF, D_BRANCH), x_prompt.dtype)

    xp, xs = x_prompt, x_sample
    outs_p, outs_s = [], []
    for l in range(DEPTH):
        lw = tuple(w[l] for w in weights)
        xp, *st_p = mixer_layer(xp, pos_p, fox_prompt, wkv_zero, shift_zero, pool_zero, conv_zero, *lw)
        attend_s = lambda q, k, v, lf, l=l: fox_sample(q, k, v, lf, cache_k, cache_v, cache_logf, page_table, l)
        xs, *st_s = mixer_layer(xs, pos_s, attend_s, state_wkv[:, l], state_shift[:, l], state_pool[:, l],
                                state_conv[:, l], *lw)
        outs_p.append(st_p)
        outs_s.append(st_s)

    def stk(outs, i):
        return jnp.stack([o[i] for o in outs], axis=1)

    y_prompt = rmsnorm(xp, final_norm_g)
    y_sample = rmsnorm(xs, final_norm_g)
    return (y_prompt, y_sample,
            stk(outs_p, 0), stk(outs_p, 1), stk(outs_p, 2), stk(outs_p, 3), stk(outs_p, 4), stk(outs_p, 5), stk(outs_p, 6),
            stk(outs_s, 0), stk(outs_s, 1), stk(outs_s, 2), stk(outs_s, 3), stk(outs_s, 4), stk(outs_s, 5), stk(outs_s, 6))
```

```python
import functools

import jax
import jax.numpy as jnp
from jax import lax
from jax.experimental import pallas as pl
from jax.experimental.pallas import tpu as pltpu

F32 = jnp.float32
BF16 = jnp.bfloat16

D_MODEL = 1024
D_BRANCH = 256
HEAD_DIM = 64
N_HEADS = 4
LORA = 64
SHIFT_W = 3 * D_BRANCH + 2 * LORA
PAGE = 128
POOL_WINDOWS = (2, 4, 8, 16)
POOL_BUF = 15
CONV_WIDTH = 31
CONV_BUF = 30
RMS_EPS = 1e-6
LN_EPS = 1e-5
GN_EPS = 64e-5
ATTN_SCALE = HEAD_DIM ** -0.5
NEG_BIG = -1e30
SAMPLE_PAD = 8
PAGES_PER_STEP = 8
VMEM_LIMIT = 56 * 1024 * 1024

NN = ((1,), (0,))
NT = ((1,), (1,))
TN = ((0,), (0,))


def _dg(a, b, dims=NN):
    return lax.dot_general(a, b, (dims, ((), ())), preferred_element_type=F32)


def _split3(x):
    hi = x.astype(BF16)
    r1 = x - hi.astype(F32)
    mid = r1.astype(BF16)
    lo = (r1 - mid.astype(F32)).astype(BF16)
    return hi, mid, lo


def _dot_f32(a, b, dims=NN):
    ah, am, _ = _split3(a)
    bh, bm, _ = _split3(b)
    return _dg(ah, bh, dims) + (_dg(ah, bm, dims) + _dg(am, bh, dims))


def _dot_exact_rhs(a, b_bf16, dims=NN):
    ah, am, al = _split3(a)
    return _dg(ah, b_bf16, dims) + (_dg(am, b_bf16, dims) + _dg(al, b_bf16, dims))


def _dot_exact_lhs(a_bf16, b, dims=NN):
    bh, bm, bl = _split3(b)
    return _dg(a_bf16, bh, dims) + (_dg(a_bf16, bm, dims) + _dg(a_bf16, bl, dims))


def _sigmoid(x):
    return 1.0 / (1.0 + jnp.exp(-x))


def _silu(x):
    return x * _sigmoid(x)


def _softplus(x):
    return jnp.maximum(x, 0.0) + jnp.log1p(jnp.exp(-jnp.abs(x)))


def _iota(shape, dim):
    return lax.broadcasted_iota(jnp.int32, shape, dim)


def _params(*sem):
    return pltpu.CompilerParams(dimension_semantics=sem, vmem_limit_bytes=VMEM_LIMIT)


_IN_SEGS = (("q", 256), ("ga", 256), ("sh", SHIFT_W), ("gb", 256),
            ("uc", 256), ("gc", 256), ("ud", 512), ("gd", 256))
_IN_COLS = sum(s for _, s in _IN_SEGS)


def _inproj_kernel(x_ref, g_ref, w_ref, wkv_ref, wf_ref, bf_ref, *out_refs):
    x = x_ref[0]
    ms = jnp.mean(x * x, axis=-1, keepdims=True)
    h = ((x * lax.rsqrt(ms + RMS_EPS)) * g_ref[...]).astype(BF16)
    off = 0
    for (_, size), o_ref in zip(_IN_SEGS, out_refs[:-3]):
        o_ref[0] = jnp.dot(h, w_ref[:, off:off + size], preferred_element_type=F32)
        off += size
    kt_ref, vt_ref, lf_ref = out_refs[-3:]
    kt_ref[0] = _dg(wkv_ref[0:D_BRANCH, :], h, NT)
    vt_ref[0] = _dg(wkv_ref[D_BRANCH:2 * D_BRANCH, :], h, NT)
    f = _dg(wf_ref[...], h, NT) + bf_ref[...]
    lf_ref[0] = -_softplus(-f)


def _inproj(x, g, w, wkv_t, wf, bf, tm):
    batch, seq, _ = x.shape
    out_shape = [jax.ShapeDtypeStruct((batch, seq, s), F32) for _, s in _IN_SEGS]
    out_shape += [jax.ShapeDtypeStruct((batch, D_BRANCH, seq), F32)] * 2
    out_shape.append(jax.ShapeDtypeStruct((batch, 8, seq), F32))
    out_specs = [pl.BlockSpec((1, tm, s), lambda b, i: (b, i, 0)) for _, s in _IN_SEGS]
    out_specs += [pl.BlockSpec((1, D_BRANCH, tm), lambda b, i: (b, 0, i))] * 2
    out_specs.append(pl.BlockSpec((1, 8, tm), lambda b, i: (b, 0, i)))

    def full(shape):
        return pl.BlockSpec(shape, lambda b, i: (0,) * len(shape))

    return pl.pallas_call(
        _inproj_kernel,
        grid=(batch, seq // tm),
        in_specs=[pl.BlockSpec((1, tm, D_MODEL), lambda b, i: (b, i, 0)),
                  full((1, D_MODEL)), full((D_MODEL, _IN_COLS)), full((2 * D_BRANCH, D_MODEL)),
                  full((8, D_MODEL)), full((8, 1))],
        out_specs=out_specs,
        out_shape=out_shape,
        compiler_params=_params("parallel", "parallel"),
        name="inproj",
    )(x, g, w, wkv_t, wf, bf)


def _fox_prompt_kernel(q_ref, kt_ref, vt_ref, lf_ref, o_ref, c_ref, *, tq, seq):
    i = pl.program_id(1)

    @pl.when(i == 0)
    def _cumsum():
        upper = (_iota((128, 128), 0) <= _iota((128, 128), 1)).astype(BF16)
        carry = jnp.zeros((8, 1), F32)
        for blk in range(seq // 128):
            cs = _dot_exact_rhs(lf_ref[0, :, blk * 128:(blk + 1) * 128], upper) + carry
            c_ref[:, blk * 128:(blk + 1) * 128] = cs
            carry = cs[:, 127:128]

    row0 = pl.multiple_of(i * tq, tq)
    c_tile = c_ref[:, pl.ds(row0, tq)]
    causal = _iota((tq, tq), 1) <= _iota((tq, tq), 0)

    for h in range(N_HEADS):
        hs = slice(h * HEAD_DIM, (h + 1) * HEAD_DIM)
        qh = (q_ref[0, :, hs] * ATTN_SCALE).astype(BF16)
        c_first = c_tile[h:h + 1, 0:1]

        def tile_update(carry, col0, masked, hs=hs, qh=qh, c_first=c_first, h=h):
            m, l, acc = carry
            kh = kt_ref[0, hs, pl.ds(col0, tq)].astype(BF16)
            vh = vt_ref[0, hs, pl.ds(col0, tq)].astype(BF16)
            s = _dg(qh, kh) + (c_first - c_ref[h:h + 1, pl.ds(col0, tq)])
            if masked:
                s = jnp.where(causal, s, NEG_BIG)
            m_new = jnp.maximum(m, jnp.max(s, axis=-1, keepdims=True))
            p = jnp.exp(s - m_new)
            alpha = jnp.exp(m - m_new)
            l = alpha * l + jnp.sum(p, axis=-1, keepdims=True)
            acc = alpha * acc + _dg(p.astype(BF16), vh, NT)
            return m_new, l, acc

        init = (jnp.full((tq, 1), NEG_BIG, F32), jnp.zeros((tq, 1), F32), jnp.zeros((tq, HEAD_DIM), F32))
        carry = lax.fori_loop(
            0, i, lambda j, c, f=tile_update: f(c, pl.multiple_of(j * tq, tq), False), init)
        m, l, acc = tile_update(carry, row0, True)
        o_ref[0, :, hs] = acc / l


def _fox_prompt(q, kt, vt, lf_t, tq):
    batch, seq, _ = q.shape
    kern = functools.partial(_fox_prompt_kernel, tq=tq, seq=seq)
    return pl.pallas_call(
        kern,
        grid=(batch, seq // tq),
        in_specs=[pl.BlockSpec((1, tq, D_BRANCH), lambda b, i: (b, i, 0)),
                  pl.BlockSpec((1, D_BRANCH, seq), lambda b, i: (b, 0, 0)),
                  pl.BlockSpec((1, D_BRANCH, seq), lambda b, i: (b, 0, 0)),
                  pl.BlockSpec((1, 8, seq), lambda b, i: (b, 0, 0))],
        out_specs=pl.BlockSpec((1, tq, D_BRANCH), lambda b, i: (b, i, 0)),
        out_shape=jax.ShapeDtypeStruct((batch, seq, D_BRANCH), F32),
        scratch_shapes=[pltpu.VMEM((8, seq), F32)],
        compiler_params=_params("parallel", "arbitrary"),
        name="fox_prompt",
    )(q, kt, vt, lf_t)


def _fox_sample_kernel(pt_ref, q_ref, kn_ref, vn_ref, lfn_ref, *refs, pages, t_valid):
    del pt_ref
    k_refs = refs[:pages]
    v_refs = refs[pages:2 * pages]
    lf_refs = refs[2 * pages:3 * pages]
    o_ref, m_ref, l_ref, acc_ref, carry_ref = refs[3 * pages:]
    rows = N_HEADS * SAMPLE_PAD
    j = pl.program_id(1)

    q_all = q_ref[0] * ATTN_SCALE
    q_heads = [q_all[:, h * HEAD_DIM:(h + 1) * HEAD_DIM].astype(BF16) for h in range(N_HEADS)]
    later = (_iota((PAGE, PAGE), 0) > _iota((PAGE, PAGE), 1)).astype(BF16)

    def head_rows(lf4):
        return jnp.concatenate(
            [jnp.broadcast_to(lf4[h:h + 1, :], (SAMPLE_PAD, PAGE)) for h in range(N_HEADS)], axis=0)

    def page_update(kt, vt, lf, mask):
        bias = _dot_exact_rhs(lf, later) + carry_ref[...]
        s = jnp.concatenate([_dg(q_heads[h], kt[h].astype(BF16)) for h in range(N_HEADS)], axis=0) + bias
        if mask is not None:
            s = jnp.where(mask, s, NEG_BIG)
        m_old = m_ref[...]
        m_new = jnp.maximum(m_old, jnp.max(s, axis=-1, keepdims=True))
        p = jnp.exp(s - m_new)
        alpha = jnp.exp(m_old - m_new)
        l_ref[...] = alpha * l_ref[...] + jnp.sum(p, axis=-1, keepdims=True)
        pv = jnp.concatenate(
            [_dg(p[h * SAMPLE_PAD:(h + 1) * SAMPLE_PAD].astype(BF16), vt[h].astype(BF16), NT)
             for h in range(N_HEADS)], axis=0)
        acc_ref[...] = alpha * acc_ref[...] + pv
        m_ref[...] = m_new
        carry_ref[...] = carry_ref[...] + jnp.sum(lf, axis=-1, keepdims=True)

    @pl.when(j == 0)
    def _new_tokens():
        m_ref[...] = jnp.full(m_ref.shape, NEG_BIG, F32)
        l_ref[...] = jnp.zeros(l_ref.shape, F32)
        acc_ref[...] = jnp.zeros(acc_ref.shape, F32)
        carry_ref[...] = jnp.zeros(carry_ref.shape, F32)
        key = _iota((rows, PAGE), 1)
        qry = _iota((rows, PAGE), 0) % SAMPLE_PAD
        lf = jnp.where(key < t_valid, head_rows(lfn_ref[0]), 0.0)
        mask = jnp.logical_and(key <= qry, key < t_valid)
        page_update(kn_ref[0], vn_ref[0], lf, mask)

    for i in range(pages):
        page_update(k_refs[i][0, 0], v_refs[i][0, 0], head_rows(lf_refs[i][0, 0]), None)

    @pl.when(j == pl.num_programs(1) - 1)
    def _finish():
        out = acc_ref[...] / l_ref[...]
        for h in range(N_HEADS):
            o_ref[0, :, h * HEAD_DIM:(h + 1) * HEAD_DIM] = out[h * SAMPLE_PAD:(h + 1) * SAMPLE_PAD]


def _fox_sample(q, kt_new, vt_new, lf_new, cache_kt, cache_vt, cache_lft, page_table, layer, t_valid):
    batch, n_pages = page_table.shape
    pages = PAGES_PER_STEP
    rows = N_HEADS * SAMPLE_PAD

    def page_map(i, ndim):
        return lambda b, j, pt: (pt[b, n_pages - 1 - (j * pages + i)], layer) + (0,) * (ndim - 2)

    def per_seq(shape):
        return pl.BlockSpec((1,) + shape, lambda b, j, pt: (b,) + (0,) * len(shape))

    kv_block = (1, 1, N_HEADS, HEAD_DIM, PAGE)
    in_specs = [per_seq((SAMPLE_PAD, D_BRANCH)), per_seq((N_HEADS, HEAD_DIM, PAGE)),
                per_seq((N_HEADS, HEAD_DIM, PAGE)), per_seq((N_HEADS, PAGE))]
    in_specs += [pl.BlockSpec(kv_block, page_map(i, 5)) for i in range(pages)]
    in_specs += [pl.BlockSpec(kv_block, page_map(i, 5)) for i in range(pages)]
    in_specs += [pl.BlockSpec((1, 1, N_HEADS, PAGE), page_map(i, 4)) for i in range(pages)]
    kern = functools.partial(_fox_sample_kernel, pages=pages, t_valid=t_valid)
    return pl.pallas_call(
        kern,
        grid_spec=pltpu.PrefetchScalarGridSpec(
            num_scalar_prefetch=1,
            grid=(batch, n_pages // pages),
            in_specs=in_specs,
            out_specs=per_seq((SAMPLE_PAD, D_BRANCH)),
            scratch_shapes=[pltpu.VMEM((rows, 1), F32), pltpu.VMEM((rows, 1), F32),
                            pltpu.VMEM((rows, HEAD_DIM), F32), pltpu.VMEM((rows, 1), F32)]),
        out_shape=jax.ShapeDtypeStruct((batch, SAMPLE_PAD, D_BRANCH), F32),
        compiler_params=_params("parallel", "arbitrary"),
        name="fox_sample",
    )(page_table, q, kt_new, vt_new, lf_new,
      *([cache_kt] * pages), *([cache_vt] * pages), *([cache_lft] * pages))


def _unit_lower_inverse(n_mat, size):
    r = _iota((size, size), 0)
    c = _iota((size, size), 1)
    eye = (r == c).astype(F32)
    m = eye - jnp.where((r // 2) == (c // 2), n_mat, 0.0)
    blk = 2
    while blk < size:
        off = jnp.logical_and((r // (2 * blk)) == (c // (2 * blk)), (r // blk) != (c // blk))
        n_off = jnp.where(off, n_mat, 0.0)
        m = m - _dot_f32(_dot_f32(m, n_off), m)
        blk *= 2
    return m


def _rwkv_kernel(sh_ref, gb_ref, s0_ref, shift0_ref, mu_ref, w0_ref, a0_ref, lora_ref, kkw_ref, ka_ref,
                 rk_ref, gng_ref, gnb_ref, o_ref, s1_ref, ext_ref, st_ref, oacc_ref, *, tb, chunk, t_valid):
    i = pl.program_id(1)

    @pl.when(i == 0)
    def _init():
        st_ref[...] = s0_ref[0]
        ext_ref[0:8, :] = jnp.zeros((8, SHIFT_W), F32)
        ext_ref[7:8, :] = shift0_ref[0]

    @pl.when(i > 0)
    def _carry_shift():
        ext_ref[7:8, :] = ext_ref[tb + 7:tb + 8, :]

    sh = sh_ref[0]
    ext_ref[8:8 + tb, :] = sh
    prev = ext_ref[7:7 + tb, :]
    xm = sh + mu_ref[...] * (prev - sh)
    r = xm[:, 0:256]
    k = xm[:, 256:512]
    v = xm[:, 512:768]
    wa = xm[:, 768:896]
    wa = jnp.where(_iota((tb, 2 * LORA), 1) < LORA, jnp.tanh(wa), wa)
    lora = jnp.dot(wa.astype(BF16), lora_ref[...], preferred_element_type=F32)
    w_log = -_softplus(-(w0_ref[...] + lora[:, 0:256])) - 0.5
    log_decay = -jnp.exp(w_log)
    a = _sigmoid(a0_ref[...] + lora[:, 256:512])

    grp = ((_iota((D_BRANCH, D_BRANCH), 0) // HEAD_DIM) == (_iota((D_BRANCH, D_BRANCH), 1) // HEAD_DIM)).astype(BF16)
    kk = k * kkw_ref[...]
    kk = kk * lax.rsqrt(jnp.maximum(_dot_exact_rhs(kk * kk, grp), 1e-12))
    kp = k * (1.0 + (a - 1.0) * ka_ref[...])
    beta = kk * a
    bonus = _dot_exact_rhs(r * kp * rk_ref[...], grp) * v
    if t_valid < tb:
        keep = (_iota((tb, 1), 0) < t_valid).astype(F32)
        log_decay = log_decay * keep
        kp = kp * keep
        beta = beta * keep

    rr = _iota((chunk, chunk), 0)
    cc = _iota((chunk, chunk), 1)
    incl = rr >= cc
    strict = rr > cc
    tri = incl.astype(BF16)

    for c in range(tb // chunk):
        sl = slice(c * chunk, (c + 1) * chunk)
        ld = log_decay[sl]
        lc = _dot_exact_lhs(tri, ld)
        g_prev = jnp.exp(lc - ld)
        g_cur = jnp.exp(lc)
        g_inv = jnp.exp(-lc)
        qc = kk[sl] * g_prev
        rc = r[sl] * g_cur
        kc = kp[sl] * g_inv
        bc = beta[sl] * g_inv
        vc = v[sl]
        g_end = g_cur[chunk - 1:chunk, :]
        for h in range(N_HEADS):
            hs = slice(h * HEAD_DIM, (h + 1) * HEAD_DIM)
            qh, rh, kh, bh, vh = qc[:, hs], rc[:, hs], kc[:, hs], bc[:, hs], vc[:, hs]
            s0 = st_ref[h]
            a_qk = jnp.where(strict, _dot_f32(qh, kh, NT), 0.0)
            a_qb = jnp.where(strict, _dot_f32(qh, bh, NT), 0.0)
            a_rk = jnp.where(incl, _dot_f32(rh, kh, NT), 0.0)
            a_rb = jnp.where(incl, _dot_f32(rh, bh, NT), 0.0)
            inv = _unit_lower_inverse(a_qb, chunk)
            u = _dot_f32(inv, _dot_f32(qh, s0, NT) + _dot_f32(a_qk, vh))
            o = _dot_f32(rh, s0, NT) + _dot_f32(a_rk, vh) - _dot_f32(a_rb, u)
            st_ref[h] = (s0 + _dot_f32(vh, kh, TN) - _dot_f32(u, bh, TN)) * g_end[:, hs]
            oacc_ref[sl, hs] = o

    o = oacc_ref[...]
    mean = _dot_exact_rhs(o, grp) * (1.0 / HEAD_DIM)
    d = o - mean
    var = _dot_exact_rhs(d * d, grp) * (1.0 / HEAD_DIM)
    o = d * lax.rsqrt(var + GN_EPS) * gng_ref[...] + gnb_ref[...] + bonus
    o_ref[0] = o * _silu(gb_ref[0])
    s1_ref[0] = st_ref[...]


def _rwkv(sh, gb, s0, shift0, w, batch, seq, tb, chunk, t_valid):
    sh3 = sh.reshape(batch, seq, SHIFT_W)
    gb3 = gb.reshape(batch, seq, D_BRANCH)
    kern = functools.partial(_rwkv_kernel, tb=tb, chunk=chunk, t_valid=t_valid)

    def row(width):
        return pl.BlockSpec((1, width), lambda b, i: (0, 0))

    state_spec = pl.BlockSpec((1, N_HEADS, HEAD_DIM, HEAD_DIM), lambda b, i: (b, 0, 0, 0))
    return pl.pallas_call(
        kern,
        grid=(batch, seq // tb),
        in_specs=[pl.BlockSpec((1, tb, SHIFT_W), lambda b, i: (b, i, 0)),
                  pl.BlockSpec((1, tb, D_BRANCH), lambda b, i: (b, i, 0)),
                  state_spec,
                  pl.BlockSpec((1, 1, SHIFT_W), lambda b, i: (b, 0, 0)),
                  row(SHIFT_W), row(D_BRANCH), row(D_BRANCH),
                  pl.BlockSpec((2 * LORA, 2 * D_BRANCH), lambda b, i: (0, 0)),
                  row(D_BRANCH), row(D_BRANCH), row(D_BRANCH), row(D_BRANCH), row(D_BRANCH)],
        out_specs=[pl.BlockSpec((1, tb, D_BRANCH), lambda b, i: (b, i, 0)), state_spec],
        out_shape=[jax.ShapeDtypeStruct((batch, seq, D_BRANCH), F32),
                   jax.ShapeDtypeStruct((batch, N_HEADS, HEAD_DIM, HEAD_DIM), F32)],
        scratch_shapes=[pltpu.VMEM((tb + 8, SHIFT_W), F32),
                        pltpu.VMEM((N_HEADS, HEAD_DIM, HEAD_DIM), F32),
                        pltpu.VMEM((tb, D_BRANCH), F32)],
        compiler_params=_params("parallel", "arbitrary"),
        name="rwkv",
    )(sh3, gb3, s0, shift0.reshape(batch, 1, SHIFT_W), w["mu"], w["w0"], w["a0"], w["lora"], w["kk"], w["ka"],
      w["rk"], w["gn_g"], w["gn_b"])


def _mix_kernel(x_ref, oa_ref, ga_ref, ob_ref, uc_ref, gc_ref, ud_ref, gd_ref, pool0_ref, conv0_ref,
                poolw_ref, pscale_ref, convw_ref, convb_ref, lng_ref, lnb_ref, pw_ref, wout_ref,
                xo_ref, pool1_ref, conv1_ref, pext_ref, cext_ref, *, tm, t_last, pos0):
    i = pl.program_id(1)
    pad_p = POOL_BUF + 1
    pad_c = CONV_BUF + 2

    @pl.when(i == 0)
    def _init():
        pext_ref[0:pad_p, :] = jnp.zeros((pad_p, D_BRANCH), F32)
        cext_ref[0:pad_c, :] = jnp.zeros((pad_c, D_BRANCH), F32)
        pext_ref[1:pad_p, :] = pool0_ref[0]
        cext_ref[2:pad_c, :] = conv0_ref[0]

    @pl.when(i > 0)
    def _carry():
        pext_ref[1:pad_p, :] = pext_ref[tm + 1:tm + pad_p, :]
        cext_ref[2:pad_c, :] = cext_ref[tm + 2:tm + pad_c, :]

    uc = uc_ref[0]
    pext_ref[pad_p:pad_p + tm, :] = uc
    ud = ud_ref[0]
    cext_ref[pad_c:pad_c + tm, :] = ud[:, 0:D_BRANCH] * _sigmoid(ud[:, D_BRANCH:2 * D_BRANCH])

    def back(d):
        return pext_ref[pad_p - d:pad_p - d + tm, :]

    win2 = uc + back(1)
    win4 = win2 + back(2) + back(3)
    win8 = win4
    for d in range(4, 8):
        win8 = win8 + back(d)
    win16 = win8
    for d in range(8, 16):
        win16 = win16 + back(d)
    group = _iota((tm, D_BRANCH), 1) // HEAD_DIM
    win = jnp.where(group == 0, win2, jnp.where(group == 1, win4, jnp.where(group == 2, win8, win16)))
    width = jnp.where(group == 0, 2, jnp.where(group == 1, 4, jnp.where(group == 2, 8, 16)))
    pos = pos0 + i * tm + _iota((tm, D_BRANCH), 0)
    cnt = jnp.minimum(pos + 1, width).astype(F32)
    pooled = win / cnt - uc
    o_c = jnp.dot(pooled.astype(BF16), poolw_ref[...], preferred_element_type=F32) * pscale_ref[...]

    y = jnp.zeros((tm, D_BRANCH), F32) + convb_ref[...]
    for jt in range(CONV_WIDTH):
        y = y + cext_ref[2 + jt:2 + jt + tm, :] * convw_ref[jt:jt + 1, :]
    mean = jnp.mean(y, axis=-1, keepdims=True)
    yc = y - mean
    var = jnp.mean(yc * yc, axis=-1, keepdims=True)
    yn = yc * lax.rsqrt(var + LN_EPS) * lng_ref[...] + lnb_ref[...]
    o_d = jnp.dot(_silu(yn).astype(BF16), pw_ref[...], preferred_element_type=F32)

    acc = jnp.dot((oa_ref[0] * _silu(ga_ref[0])).astype(BF16), wout_ref[0:256, :], preferred_element_type=F32)
    acc = acc + jnp.dot(ob_ref[0].astype(BF16), wout_ref[256:512, :], preferred_element_type=F32)
    acc = acc + jnp.dot((o_c * _silu(gc_ref[0])).astype(BF16), wout_ref[512:768, :], preferred_element_type=F32)
    acc = acc + jnp.dot((o_d * _silu(gd_ref[0])).astype(BF16), wout_ref[768:1024, :], preferred_element_type=F32)
    xo_ref[0] = x_ref[0] + acc

    pool1_ref[0] = pext_ref[t_last + 1:t_last + pad_p, :]
    conv1_ref[0] = cext_ref[t_last + 2:t_last + pad_c, :]


def _mix(x, oa, ga, ob, uc, gc, ud, gd, pool0, conv0, w, batch, seq, tm, t_last, pos0):
    def tok(t, width):
        return t.reshape(batch, seq, width)

    def tok_spec(width):
        return pl.BlockSpec((1, tm, width), lambda b, i: (b, i, 0))

    def full(shape):
        return pl.BlockSpec(shape, lambda b, i: (0,) * len(shape))

    pool_spec = pl.BlockSpec((1, POOL_BUF, D_BRANCH), lambda b, i: (b, 0, 0))
    conv_spec = pl.BlockSpec((1, CONV_BUF, D_BRANCH), lambda b, i: (b, 0, 0))
    kern = functools.partial(_mix_kernel, tm=tm, t_last=t_last, pos0=pos0)
    return pl.pallas_call(
        kern,
        grid=(batch, seq // tm),
        in_specs=[tok_spec(D_MODEL), tok_spec(256), tok_spec(256), tok_spec(256), tok_spec(256), tok_spec(256),
                  tok_spec(512), tok_spec(256), pool_spec, conv_spec,
                  full((D_BRANCH, D_BRANCH)), full((1, D_BRANCH)), full((CONV_WIDTH, D_BRANCH)),
                  full((1, D_BRANCH)), full((1, D_BRANCH)), full((1, D_BRANCH)),
                  full((D_BRANCH, D_BRANCH)), full((D_MODEL, D_MODEL))],
        out_specs=[tok_spec(D_MODEL), pool_spec, conv_spec],
        out_shape=[jax.ShapeDtypeStruct((batch, seq, D_MODEL), F32),
                   jax.ShapeDtypeStruct((batch, POOL_BUF, D_BRANCH), F32),
                   jax.ShapeDtypeStruct((batch, CONV_BUF, D_BRANCH), F32)],
        scratch_shapes=[pltpu.VMEM((POOL_BUF + 1 + tm, D_BRANCH), F32),
                        pltpu.VMEM((CONV_BUF + 2 + tm, D_BRANCH), F32)],
        compiler_params=_params("parallel", "arbitrary"),
        name="mix",
    )(tok(x, D_MODEL), tok(oa, 256), tok(ga, 256), tok(ob, 256), tok(uc, 256), tok(gc, 256), tok(ud, 512),
      tok(gd, 256), pool0, conv0, w["pool_w"], w["pool_scale"], w["conv_w"], w["conv_b"], w["ln_g"], w["ln_b"],
      w["pw_out"], w["w_out"])


def _final_norm_kernel(x_ref, g_ref, o_ref):
    x = x_ref[...]
    ms = jnp.mean(x * x, axis=-1, keepdims=True)
    o_ref[...] = (x * lax.rsqrt(ms + RMS_EPS)) * g_ref[...]


def _final_norm(x, g, tm):
    n = x.shape[0]
    return pl.pallas_call(
        _final_norm_kernel,
        grid=(n // tm,),
        in_specs=[pl.BlockSpec((tm, D_MODEL), lambda i: (i, 0)), pl.BlockSpec((1, D_MODEL), lambda i: (0, 0))],
        out_specs=pl.BlockSpec((tm, D_MODEL), lambda i: (i, 0)),
        out_shape=jax.ShapeDtypeStruct((n, D_MODEL), F32),
        compiler_params=_params("parallel"),
        name="final_norm",
    )(x, g)


def _layer_weights(l, norm_g, w_in, fox_bf, rw_mu, rw_w0, rw_wup, rw_a0, rw_aup, rw_kk, rw_ka, rw_rk,
                   rw_gn_g, rw_gn_b, pool_w, pool_scale, conv_w, conv_b, ln_g, ln_b, pw_out, w_out):
    wi = w_in[l]
    o_f = 3 * D_BRANCH
    o_ga = o_f + N_HEADS
    main = jnp.concatenate([wi[:, 0:D_BRANCH], wi[:, o_ga:]], axis=1).astype(BF16)
    wkv_t = wi[:, D_BRANCH:o_f].T.astype(BF16)
    wf = jnp.zeros((8, D_MODEL), F32).at[0:N_HEADS].set(wi[:, o_f:o_ga].T).astype(BF16)
    bf = jnp.zeros((8, 1), F32).at[0:N_HEADS, 0].set(fox_bf[l])
    lora = jnp.zeros((2 * LORA, 2 * D_BRANCH), F32)
    lora = lora.at[0:LORA, 0:D_BRANCH].set(rw_wup[l]).at[LORA:, D_BRANCH:].set(rw_aup[l]).astype(BF16)
    pw_bd = jnp.zeros((D_BRANCH, D_BRANCH), F32)
    for g in range(len(POOL_WINDOWS)):
        gs = slice(g * HEAD_DIM, (g + 1) * HEAD_DIM)
        pw_bd = pw_bd.at[gs, gs].set(pool_w[l, g])

    def row(t):
        return t[l].reshape(1, -1)

    return dict(
        norm_g=row(norm_g), main=main, wkv_t=wkv_t, wf=wf, bf=bf,
        rwkv=dict(mu=row(rw_mu), w0=row(rw_w0), a0=row(rw_a0), lora=lora, kk=row(rw_kk), ka=row(rw_ka),
                  rk=row(rw_rk), gn_g=row(rw_gn_g), gn_b=row(rw_gn_b)),
        mix=dict(pool_w=pw_bd.astype(BF16), pool_scale=row(pool_scale), conv_w=conv_w[l], conv_b=row(conv_b),
                 ln_g=row(ln_g), ln_b=row(ln_b), pw_out=pw_out[l].astype(BF16), w_out=w_out[l].astype(BF16)))


def _stream_layer(x, lw, attend, wkv0, shift0, pool0, conv0, tiles, t_valid, pos0, flat_inproj):
    batch, seq, _ = x.shape
    x_in = x.reshape(1, batch * seq, D_MODEL) if flat_inproj else x
    *tok, kt, vt, lf_t = _inproj(x_in, lw["norm_g"], lw["main"], lw["wkv_t"], lw["wf"], lw["bf"], tiles["inproj"])
    q, ga, sh, gb, uc, gc, ud, gd = (t.reshape(batch, seq, t.shape[-1]) for t in tok)
    oa = attend(q, kt, vt, lf_t)
    ob, wkv1 = _rwkv(sh, gb, wkv0, shift0, lw["rwkv"], batch, seq, tiles["rwkv"], tiles["chunk"], t_valid)
    t_last = t_valid - (seq - tiles["mix"])
    x1, pool1, conv1 = _mix(x, oa, ga, ob, uc, gc, ud, gd, pool0, conv0, lw["mix"], batch, seq, tiles["mix"],
                            t_last, pos0)
    return x1, kt, vt, sh, lf_t, wkv1, pool1, conv1


def kernel(x_prompt, x_sample, cache_k, cache_v, cache_logf, state_wkv, state_shift, state_pool, state_conv,
           page_table, norm_g, w_in, fox_bf, rw_mu, rw_w0, rw_wup, rw_a0, rw_aup, rw_kk, rw_ka, rw_rk,
           rw_gn_g, rw_gn_b, pool_w, pool_scale, conv_w, conv_b, ln_g, ln_b, pw_out, w_out, final_norm_g):
    bp, tp, _ = x_prompt.shape
    bs, ts, _ = x_sample.shape
    depth = w_in.shape[0]
    n_phys = cache_k.shape[0]
    past = page_table.shape[1] * PAGE
    weights = (norm_g, w_in, fox_bf, rw_mu, rw_w0, rw_wup, rw_a0, rw_aup, rw_kk, rw_ka, rw_rk, rw_gn_g, rw_gn_b,
               pool_w, pool_scale, conv_w, conv_b, ln_g, ln_b, pw_out, w_out)

    tile_p = min(256, tp)
    tiles_p = dict(inproj=tile_p, rwkv=tile_p, chunk=min(64, tile_p), mix=tile_p)
    tiles_s = dict(inproj=min(256, bs * SAMPLE_PAD), rwkv=SAMPLE_PAD, chunk=SAMPLE_PAD, mix=SAMPLE_PAD)

    xp = x_prompt
    xs = jnp.pad(x_sample, ((0, 0), (0, SAMPLE_PAD - ts), (0, 0)))
    cache_kt = jnp.transpose(cache_k, (0, 1, 3, 4, 2))
    cache_vt = jnp.transpose(cache_v, (0, 1, 3, 4, 2))
    cache_lft = jnp.transpose(cache_logf, (0, 1, 3, 2))

    wkv_zero = jnp.zeros((bp, N_HEADS, HEAD_DIM, HEAD_DIM), F32)
    shift_zero = jnp.zeros((bp, SHIFT_W), F32)
    pool_zero = jnp.zeros((bp, POOL_BUF, D_BRANCH), F32)
    conv_zero = jnp.zeros((bp, CONV_BUF, D_BRANCH), F32)

    def sample_heads(t):
        return t.reshape(N_HEADS, HEAD_DIM, bs, SAMPLE_PAD).transpose(2, 0, 1, 3)

    def pad_keys(t):
        return jnp.pad(t, [(0, 0)] * (t.ndim - 1) + [(0, PAGE - SAMPLE_PAD)])

    outs_p, outs_s = [], []
    for l in range(depth):
        lw = _layer_weights(l, *weights)

        attend_p = lambda q, kt, vt, lf_t: _fox_prompt(q, kt, vt, lf_t, tile_p)
        xp, kt_p, vt_p, sh_p, lf_p, wkv_p, pool_p, conv_p = _stream_layer(
            xp, lw, attend_p, wkv_zero, shift_zero, pool_zero, conv_zero, tiles_p, tp, 0, False)
        outs_p.append((kt_p.reshape(bp, N_HEADS, HEAD_DIM, tp), vt_p.reshape(bp, N_HEADS, HEAD_DIM, tp),
                       lf_p[:, 0:N_HEADS], wkv_p, sh_p[:, tp - 1], pool_p, conv_p))

        def attend_s(q, kt, vt, lf_t, l=l):
            lf_new = lf_t[0, 0:N_HEADS].reshape(N_HEADS, bs, SAMPLE_PAD).transpose(1, 0, 2)
            return _fox_sample(q, pad_keys(sample_heads(kt)), pad_keys(sample_heads(vt)), pad_keys(lf_new),
                               cache_kt, cache_vt, cache_lft, page_table, l, ts)

        xs, kt_s, vt_s, sh_s, lf_s, wkv_s, pool_s, conv_s = _stream_layer(
            xs, lw, attend_s, state_wkv[:, l], state_shift[:, l], state_pool[:, l], state_conv[:, l],
            tiles_s, ts, past, True)
        outs_s.append((sample_heads(kt_s).transpose(0, 3, 1, 2)[:, 0:ts],
                       sample_heads(vt_s).transpose(0, 3, 1, 2)[:, 0:ts],
                       lf_s[0, 0:N_HEADS].reshape(N_HEADS, bs, SAMPLE_PAD).transpose(1, 2, 0)[:, 0:ts],
                       wkv_s, sh_s[:, ts - 1], pool_s, conv_s))

    fg = final_norm_g.reshape(1, D_MODEL)
    y_prompt = _final_norm(xp.reshape(bp * tp, D_MODEL), fg, tile_p).reshape(bp, tp, D_MODEL)
    y_sample = _final_norm(xs.reshape(bs * SAMPLE_PAD, D_MODEL), fg, min(256, bs * SAMPLE_PAD))
    y_sample = y_sample.reshape(bs, SAMPLE_PAD, D_MODEL)[:, 0:ts]

    def stk(outs, i):
        return jnp.stack([o[i] for o in outs], axis=1)

    k_prompt = stk(outs_p, 0).transpose(0, 1, 4, 2, 3)
    v_prompt = stk(outs_p, 1).transpose(0, 1, 4, 2, 3)
    logf_prompt = stk(outs_p, 2).transpose(0, 1, 3, 2)
    return (y_prompt, y_sample, k_prompt, v_prompt, logf_prompt,
            *(stk(outs_p, i) for i in range(3, 7)),
            *(stk(outs_s, i) for i in range(7)))
```

```python
import functools

import jax
import jax.numpy as jnp
from jax import lax
from jax.experimental import pallas as pl
from jax.experimental.pallas import tpu as pltpu

F32 = jnp.float32
BF16 = jnp.bfloat16

D_MODEL = 1024
D_BRANCH = 256
HEAD_DIM = 64
N_HEADS = 4
LORA = 64
SHIFT_W = 3 * D_BRANCH + 2 * LORA
PAGE = 128
POOL_WINDOWS = (2, 4, 8, 16)
POOL_BUF = 15
CONV_WIDTH = 31
CONV_BUF = 30
RMS_EPS = 1e-6
LN_EPS = 1e-5
GN_EPS = 64e-5
ATTN_SCALE = HEAD_DIM ** -0.5
NEG_BIG = -1e30
SAMPLE_PAD = 8
PAGES_PER_STEP = 8
VMEM_LIMIT = 56 * 1024 * 1024

NN = ((1,), (0,))
NT = ((1,), (1,))
TN = ((0,), (0,))


def _dg(a, b, dims=NN):
    return lax.dot_general(a, b, (dims, ((), ())), preferred_element_type=F32)


def _split3(x):
    hi = x.astype(BF16)
    r1 = x - hi.astype(F32)
    mid = r1.astype(BF16)
    lo = (r1 - mid.astype(F32)).astype(BF16)
    return hi, mid, lo


def _dot_f32(a, b, dims=NN):
    ah, am, _ = _split3(a)
    bh, bm, _ = _split3(b)
    return _dg(ah, bh, dims) + (_dg(ah, bm, dims) + _dg(am, bh, dims))


def _dot_exact_rhs(a, b_bf16, dims=NN):
    ah, am, al = _split3(a)
    return _dg(ah, b_bf16, dims) + (_dg(am, b_bf16, dims) + _dg(al, b_bf16, dims))


def _dot_exact_lhs(a_bf16, b, dims=NN):
    bh, bm, bl = _split3(b)
    return _dg(a_bf16, bh, dims) + (_dg(a_bf16, bm, dims) + _dg(a_bf16, bl, dims))


def _group_sum(a, ones_bf16):
    hi = a.astype(BF16)
    mid = (a - hi.astype(F32)).astype(BF16)
    return _dg(hi, ones_bf16) + _dg(mid, ones_bf16)


def _sigmoid(x):
    return 1.0 / (1.0 + jnp.exp(-x))


def _silu(x):
    return x * _sigmoid(x)


def _softplus(x):
    return jnp.maximum(x, 0.0) + jnp.log1p(jnp.exp(-jnp.abs(x)))


def _iota(shape, dim):
    return lax.broadcasted_iota(jnp.int32, shape, dim)


def _params(*sem):
    return pltpu.CompilerParams(dimension_semantics=sem, vmem_limit_bytes=VMEM_LIMIT)


_IN_SEGS = (("q", 256), ("ga", 256), ("sh", SHIFT_W), ("gb", 256),
            ("uc", 256), ("gc", 256), ("ud", 512), ("gd", 256))
_IN_COLS = sum(s for _, s in _IN_SEGS)


def _inproj_kernel(x_ref, g_ref, w_ref, wkv_ref, wf_ref, bf_ref, *out_refs):
    x = x_ref[0]
    ms = jnp.mean(x * x, axis=-1, keepdims=True)
    h = ((x * lax.rsqrt(ms + RMS_EPS)) * g_ref[...]).astype(BF16)
    off = 0
    for (_, size), o_ref in zip(_IN_SEGS, out_refs[:-3]):
        o_ref[0] = jnp.dot(h, w_ref[:, off:off + size], preferred_element_type=F32)
        off += size
    kt_ref, vt_ref, lf_ref = out_refs[-3:]
    kt_ref[0] = _dg(wkv_ref[0:D_BRANCH, :], h, NT)
    vt_ref[0] = _dg(wkv_ref[D_BRANCH:2 * D_BRANCH, :], h, NT)
    f = _dg(wf_ref[...], h, NT) + bf_ref[...]
    lf_ref[0] = -_softplus(-f)


def _inproj(x, g, w, wkv_t, wf, bf, tm):
    batch, seq, _ = x.shape
    out_shape = [jax.ShapeDtypeStruct((batch, seq, s), F32) for _, s in _IN_SEGS]
    out_shape += [jax.ShapeDtypeStruct((batch, D_BRANCH, seq), F32)] * 2
    out_shape.append(jax.ShapeDtypeStruct((batch, 8, seq), F32))
    out_specs = [pl.BlockSpec((1, tm, s), lambda b, i: (b, i, 0)) for _, s in _IN_SEGS]
    out_specs += [pl.BlockSpec((1, D_BRANCH, tm), lambda b, i: (b, 0, i))] * 2
    out_specs.append(pl.BlockSpec((1, 8, tm), lambda b, i: (b, 0, i)))

    def full(shape):
        return pl.BlockSpec(shape, lambda b, i: (0,) * len(shape))

    return pl.pallas_call(
        _inproj_kernel,
        grid=(batch, seq // tm),
        in_specs=[pl.BlockSpec((1, tm, D_MODEL), lambda b, i: (b, i, 0)),
                  full((1, D_MODEL)), full((D_MODEL, _IN_COLS)), full((2 * D_BRANCH, D_MODEL)),
                  full((8, D_MODEL)), full((8, 1))],
        out_specs=out_specs,
        out_shape=out_shape,
        compiler_params=_params("parallel", "parallel"),
        name="inproj",
    )(x, g, w, wkv_t, wf, bf)


def _fox_prompt_kernel(q_ref, kt_ref, vt_ref, lf_ref, o_ref, c_ref, kb_ref, vb_ref, m_ref, acc_ref, *, tq, seq):
    i = pl.program_id(1)
    heads = [slice(h * HEAD_DIM, (h + 1) * HEAD_DIM) for h in range(N_HEADS)]

    @pl.when(i == 0)
    def _per_sequence():
        upper = (_iota((128, 128), 0) <= _iota((128, 128), 1)).astype(BF16)
        carry = jnp.zeros((8, 1), F32)
        for blk in range(seq // 128):
            cs = _dot_exact_rhs(lf_ref[0, :, blk * 128:(blk + 1) * 128], upper) + carry
            c_ref[:, blk * 128:(blk + 1) * 128] = cs
            carry = cs[:, 127:128]
        kb_ref[...] = kt_ref[0].astype(BF16)
        for h, hs in enumerate(heads):
            vb_ref[h, 0:HEAD_DIM, :] = vt_ref[0, hs, :].astype(BF16)
            vb_ref[h, HEAD_DIM:2 * HEAD_DIM, :] = jnp.ones((HEAD_DIM, seq), BF16)

    row0 = pl.multiple_of(i * tq, tq)
    c_tile = c_ref[:, pl.ds(row0, tq)]
    causal = _iota((tq, tq), 1) <= _iota((tq, tq), 0)

    q_all = q_ref[0] * ATTN_SCALE
    q_heads = [q_all[:, hs].astype(BF16) for hs in heads]
    m_ref[...] = jnp.full(m_ref.shape, NEG_BIG, F32)
    acc_ref[...] = jnp.zeros(acc_ref.shape, F32)

    def tile_update(col0, masked):
        for h, hs in enumerate(heads):
            kh = kb_ref[hs, pl.ds(col0, tq)]
            vh = vb_ref[h, :, pl.ds(col0, tq)]
            s = _dg(q_heads[h], kh) + (c_tile[h:h + 1, 0:1] - c_ref[h:h + 1, pl.ds(col0, tq)])
            if masked:
                s = jnp.where(causal, s, NEG_BIG)
            m_old = m_ref[h]
            m_new = jnp.maximum(m_old, jnp.max(s, axis=-1, keepdims=True))
            p = jnp.exp(s - jnp.tile(m_new, (1, tq // 128)))
            alpha = jnp.exp(m_old - m_new)
            acc_ref[h] = alpha * acc_ref[h] + _dg(p.astype(BF16), vh, NT)
            m_ref[h] = m_new

    def body(j, carry):
        tile_update(pl.multiple_of(j * tq, tq), False)
        return carry

    lax.fori_loop(0, i, body, 0)
    tile_update(row0, True)
    for h, hs in enumerate(heads):
        a = acc_ref[h]
        o_ref[0, :, hs] = a[:, 0:HEAD_DIM] / a[:, HEAD_DIM:2 * HEAD_DIM]


def _fox_prompt(q, kt, vt, lf_t, tq):
    batch, seq, _ = q.shape
    kern = functools.partial(_fox_prompt_kernel, tq=tq, seq=seq)
    return pl.pallas_call(
        kern,
        grid=(batch, seq // tq),
        in_specs=[pl.BlockSpec((1, tq, D_BRANCH), lambda b, i: (b, i, 0)),
                  pl.BlockSpec((1, D_BRANCH, seq), lambda b, i: (b, 0, 0)),
                  pl.BlockSpec((1, D_BRANCH, seq), lambda b, i: (b, 0, 0)),
                  pl.BlockSpec((1, 8, seq), lambda b, i: (b, 0, 0))],
        out_specs=pl.BlockSpec((1, tq, D_BRANCH), lambda b, i: (b, i, 0)),
        out_shape=jax.ShapeDtypeStruct((batch, seq, D_BRANCH), F32),
        scratch_shapes=[pltpu.VMEM((8, seq), F32),
                        pltpu.VMEM((D_BRANCH, seq), BF16), pltpu.VMEM((N_HEADS, 2 * HEAD_DIM, seq), BF16),
                        pltpu.VMEM((N_HEADS, tq, 128), F32), pltpu.VMEM((N_HEADS, tq, 2 * HEAD_DIM), F32)],
        compiler_params=_params("parallel", "arbitrary"),
        name="fox_prompt",
    )(q, kt, vt, lf_t)


def _fox_sample_kernel(pt_ref, q_ref, kn_ref, vn_ref, lfn_ref, *refs, pages, t_valid):
    del pt_ref
    k_refs = refs[:pages]
    v_refs = refs[pages:2 * pages]
    lf_refs = refs[2 * pages:3 * pages]
    o_ref, m_ref, l_ref, acc_ref, carry_ref = refs[3 * pages:]
    rows = N_HEADS * SAMPLE_PAD
    j = pl.program_id(1)

    q_all = q_ref[0] * ATTN_SCALE
    q_heads = [q_all[:, h * HEAD_DIM:(h + 1) * HEAD_DIM].astype(BF16) for h in range(N_HEADS)]
    later = (_iota((PAGE, PAGE), 0) > _iota((PAGE, PAGE), 1)).astype(BF16)

    def head_rows(lf4):
        return jnp.concatenate(
            [jnp.broadcast_to(lf4[h:h + 1, :], (SAMPLE_PAD, PAGE)) for h in range(N_HEADS)], axis=0)

    def pages_update(kts, vts, lfs, mask):
        running = carry_ref[...]
        scores = []
        for kt, lf in zip(kts, lfs):
            qk = jnp.concatenate([_dg(q_heads[h], kt[h].astype(BF16)) for h in range(N_HEADS)], axis=0)
            scores.append(qk + (_dot_exact_rhs(lf, later) + running))
            running = running + jnp.sum(lf, axis=-1, keepdims=True)
        carry_ref[...] = running
        s = jnp.concatenate(scores, axis=1) if len(scores) > 1 else scores[0]
        if mask is not None:
            s = jnp.where(mask, s, NEG_BIG)
        m_old = m_ref[...]
        m_new = jnp.maximum(m_old, jnp.max(s, axis=-1, keepdims=True))
        p = jnp.exp(s - m_new)
        alpha = jnp.exp(m_old - m_new)
        l_ref[...] = alpha * l_ref[...] + jnp.sum(p, axis=-1, keepdims=True)
        pv = []
        for h in range(N_HEADS):
            ph = p[h * SAMPLE_PAD:(h + 1) * SAMPLE_PAD]
            acc_h = None
            for i, vt in enumerate(vts):
                t = _dg(ph[:, i * PAGE:(i + 1) * PAGE].astype(BF16), vt[h].astype(BF16), NT)
                acc_h = t if acc_h is None else acc_h + t
            pv.append(acc_h)
        acc_ref[...] = alpha * acc_ref[...] + jnp.concatenate(pv, axis=0)
        m_ref[...] = m_new

    @pl.when(j == 0)
    def _new_tokens():
        m_ref[...] = jnp.full(m_ref.shape, NEG_BIG, F32)
        l_ref[...] = jnp.zeros(l_ref.shape, F32)
        acc_ref[...] = jnp.zeros(acc_ref.shape, F32)
        carry_ref[...] = jnp.zeros(carry_ref.shape, F32)
        key = _iota((rows, PAGE), 1)
        qry = _iota((rows, PAGE), 0) % SAMPLE_PAD
        lf = jnp.where(key < t_valid, head_rows(lfn_ref[0]), 0.0)
        mask = jnp.logical_and(key <= qry, key < t_valid)
        pages_update([kn_ref[0]], [vn_ref[0]], [lf], mask)

    pages_update([k_refs[i][0, 0] for i in range(pages)], [v_refs[i][0, 0] for i in range(pages)],
                 [head_rows(lf_refs[i][0, 0]) for i in range(pages)], None)

    @pl.when(j == pl.num_programs(1) - 1)
    def _finish():
        out = acc_ref[...] / l_ref[...]
        for h in range(N_HEADS):
            o_ref[0, :, h * HEAD_DIM:(h + 1) * HEAD_DIM] = out[h * SAMPLE_PAD:(h + 1) * SAMPLE_PAD]


def _fox_sample(q, kt_new, vt_new, lf_new, cache_kt, cache_vt, cache_lft, page_table, layer, t_valid):
    batch, n_pages = page_table.shape
    pages = PAGES_PER_STEP
    rows = N_HEADS * SAMPLE_PAD

    def page_map(i, ndim):
        return lambda b, j, pt: (pt[b, n_pages - 1 - (j * pages + i)], layer) + (0,) * (ndim - 2)

    def per_seq(shape):
        return pl.BlockSpec((1,) + shape, lambda b, j, pt: (b,) + (0,) * len(shape))

    kv_block = (1, 1, N_HEADS, HEAD_DIM, PAGE)
    in_specs = [per_seq((SAMPLE_PAD, D_BRANCH)), per_seq((N_HEADS, HEAD_DIM, PAGE)),
                per_seq((N_HEADS, HEAD_DIM, PAGE)), per_seq((N_HEADS, PAGE))]
    in_specs += [pl.BlockSpec(kv_block, page_map(i, 5)) for i in range(pages)]
    in_specs += [pl.BlockSpec(kv_block, page_map(i, 5)) for i in range(pages)]
    in_specs += [pl.BlockSpec((1, 1, N_HEADS, PAGE), page_map(i, 4)) for i in range(pages)]
    kern = functools.partial(_fox_sample_kernel, pages=pages, t_valid=t_valid)
    return pl.pallas_call(
        kern,
        grid_spec=pltpu.PrefetchScalarGridSpec(
            num_scalar_prefetch=1,
            grid=(batch, n_pages // pages),
            in_specs=in_specs,
            out_specs=per_seq((SAMPLE_PAD, D_BRANCH)),
            scratch_shapes=[pltpu.VMEM((rows, 1), F32), pltpu.VMEM((rows, 1), F32),
                            pltpu.VMEM((rows, HEAD_DIM), F32), pltpu.VMEM((rows, 1), F32)]),
        out_shape=jax.ShapeDtypeStruct((batch, SAMPLE_PAD, D_BRANCH), F32),
        compiler_params=_params("parallel", "arbitrary"),
        name="fox_sample",
    )(page_table, q, kt_new, vt_new, lf_new,
      *([cache_kt] * pages), *([cache_vt] * pages), *([cache_lft] * pages))


def _unit_lower_inverse(n_bf16, size, chunk):
    r = _iota((size, size), 0)
    c = _iota((size, size), 1)
    eye = (r == c).astype(F32)
    m = eye - jnp.where((r // 2) == (c // 2), n_bf16.astype(F32), 0.0)
    blk = 2
    while blk < chunk:
        off = jnp.logical_and((r // (2 * blk)) == (c // (2 * blk)), (r // blk) != (c // blk))
        mb = m.astype(BF16)
        t = _dg(_dg(mb, n_bf16).astype(BF16), mb)
        m = m - jnp.where(off, t, 0.0)
        blk *= 2
    return m


def _rwkv_kernel(sh_ref, gb_ref, s0_ref, prev0_ref, mu_ref, w0_ref, a0_ref, lora_ref, kkw_ref, ka_ref,
                 rk_ref, gng_ref, gnb_ref, o_ref, s1_ref, ext_ref, st_ref, oacc_ref, *, tb, chunk, t_valid,
                 carried):
    n_blk = tb // chunk
    sh = sh_ref[0]
    if carried:
        i = pl.program_id(1)

        @pl.when(i == 0)
        def _init():
            st_ref[...] = s0_ref[0]
            ext_ref[0:8, :] = jnp.zeros((8, SHIFT_W), F32)
            ext_ref[7:8, :] = prev0_ref[0]

        @pl.when(i > 0)
        def _carry_shift():
            ext_ref[7:8, :] = ext_ref[tb + 7:tb + 8, :]

        ext_ref[8:8 + tb, :] = sh
        prev = ext_ref[7:7 + tb, :]
    else:
        ext_ref[0:8, :] = jnp.zeros((8, SHIFT_W), F32)
        ext_ref[8:8 + tb, :] = sh
        first = (_iota((tb, 1), 0) % chunk) == 0
        prev = jnp.where(first, prev0_ref[0], ext_ref[7:7 + tb, :])
    xm = sh + mu_ref[...] * (prev - sh)
    r = xm[:, 0:256]
    k = xm[:, 256:512]
    v = xm[:, 512:768]
    wa = xm[:, 768:896]
    wa = jnp.where(_iota((tb, 2 * LORA), 1) < LORA, jnp.tanh(wa), wa)
    lora = jnp.dot(wa.astype(BF16), lora_ref[...], preferred_element_type=F32)
    w_log = -_softplus(-(w0_ref[...] + lora[:, 0:256])) - 0.5
    log_decay = -jnp.exp(w_log)
    a = _sigmoid(a0_ref[...] + lora[:, 256:512])

    grp = ((_iota((D_BRANCH, D_BRANCH), 0) // HEAD_DIM) == (_iota((D_BRANCH, D_BRANCH), 1) // HEAD_DIM)).astype(BF16)
    kk = k * kkw_ref[...]
    kk = kk * lax.rsqrt(jnp.maximum(_group_sum(kk * kk, grp), 1e-12))
    kp = k * (1.0 + (a - 1.0) * ka_ref[...])
    beta = kk * a
    bonus = _group_sum(r * kp * rk_ref[...], grp) * v
    if t_valid < chunk:
        keep = ((_iota((tb, 1), 0) % chunk) < t_valid).astype(F32)
        log_decay = log_decay * keep
        kp = kp * keep
        beta = beta * keep

    rr = _iota((tb, tb), 0)
    cc = _iota((tb, tb), 1)
    same_block = (rr // chunk) == (cc // chunk)
    incl = jnp.logical_and(same_block, rr >= cc)
    strict = jnp.logical_and(same_block, rr > cc)

    lc = _dot_exact_lhs(incl.astype(BF16), log_decay)
    lc_tot = jnp.concatenate(
        [jnp.broadcast_to(lc[(c + 1) * chunk - 1:(c + 1) * chunk, :], (chunk, D_BRANCH)) for c in range(n_blk)],
        axis=0)
    to_end = jnp.exp(lc_tot - lc)
    g_inv = jnp.exp(-lc)
    g_end = jnp.exp(lc_tot)
    qc = kk * jnp.exp(lc - log_decay)
    rc = r * jnp.exp(lc)
    kc = kp * g_inv
    bc = beta * g_inv
    kd = kp * to_end
    bd = beta * to_end
    eye_k = (_iota((HEAD_DIM, HEAD_DIM), 0) == _iota((HEAD_DIM, HEAD_DIM), 1)).astype(F32)

    def b16(t):
        return t.astype(BF16)

    for h in range(N_HEADS):
        hs = slice(h * HEAD_DIM, (h + 1) * HEAD_DIM)
        rh, vh, kdh, bdh = rc[:, hs], v[:, hs], kd[:, hs], bd[:, hs]
        qh, kh, bh, rhb, vhb = b16(qc[:, hs]), b16(kc[:, hs]), b16(bc[:, hs]), b16(rh), b16(vh)
        a_qk = b16(jnp.where(strict, _dg(qh, kh, NT), 0.0))
        a_qb = b16(jnp.where(strict, _dg(qh, bh, NT), 0.0))
        a_rk = b16(jnp.where(incl, _dg(rhb, kh, NT), 0.0))
        a_rb = b16(jnp.where(incl, _dg(rhb, bh, NT), 0.0))
        inv = b16(_unit_lower_inverse(a_qb, tb, chunk))
        wm = _dg(inv, qh)
        ym = _dg(inv, b16(_dg(a_qk, vhb)))
        o0 = _dg(a_rk, vhb) - _dg(a_rb, b16(ym))
        rp = rh - _dg(a_rb, b16(wm))
        state = st_ref[h] if carried else None
        for c in range(n_blk):
            sl = slice(c * chunk, (c + 1) * chunk)
            bd_c = b16(bdh[sl])
            p_mat = eye_k * g_end[c * chunk:c * chunk + 1, hs] - _dg(b16(wm[sl]), bd_c, TN)
            z_mat = _dg(b16(vh[sl]), b16(kdh[sl]), TN) - _dg(b16(ym[sl]), bd_c, TN)
            s_in = state if carried else s0_ref[c, h]
            oacc_ref[sl, hs] = o0[sl] + _dg(b16(rp[sl]), b16(s_in), NT)
            s_out = _dot_f32(s_in, p_mat) + z_mat
            if carried:
                state = s_out
            else:
                s1_ref[c, h] = s_out
        if carried:
            st_ref[h] = state

    o = oacc_ref[...]
    mean = _group_sum(o, grp) * (1.0 / HEAD_DIM)
    d = o - mean
    var = _group_sum(d * d, grp) * (1.0 / HEAD_DIM)
    o = d * lax.rsqrt(var + GN_EPS) * gng_ref[...] + gnb_ref[...] + bonus
    o_ref[0] = o * _silu(gb_ref[0])
    if carried:
        s1_ref[0] = st_ref[...]


def _rwkv(sh, gb, s0, prev0, w, tb, chunk, t_valid, carried):
    batch, seq, _ = sh.shape
    kern = functools.partial(_rwkv_kernel, tb=tb, chunk=chunk, t_valid=t_valid, carried=carried)

    def row(width):
        return pl.BlockSpec((1, width), lambda b, i: (0, 0))

    if carried:
        state_spec = pl.BlockSpec((1, N_HEADS, HEAD_DIM, HEAD_DIM), lambda b, i: (b, 0, 0, 0))
        prev_spec = pl.BlockSpec((1, 1, SHIFT_W), lambda b, i: (b, 0, 0))
    else:
        state_spec = pl.BlockSpec((tb // chunk, N_HEADS, HEAD_DIM, HEAD_DIM), lambda b, i: (i, 0, 0, 0))
        prev_spec = pl.BlockSpec((1, tb, SHIFT_W), lambda b, i: (b, i, 0))
    return pl.pallas_call(
        kern,
        grid=(batch, seq // tb),
        in_specs=[pl.BlockSpec((1, tb, SHIFT_W), lambda b, i: (b, i, 0)),
                  pl.BlockSpec((1, tb, D_BRANCH), lambda b, i: (b, i, 0)),
                  state_spec, prev_spec,
                  row(SHIFT_W), row(D_BRANCH), row(D_BRANCH),
                  pl.BlockSpec((2 * LORA, 2 * D_BRANCH), lambda b, i: (0, 0)),
                  row(D_BRANCH), row(D_BRANCH), row(D_BRANCH), row(D_BRANCH), row(D_BRANCH)],
        out_specs=[pl.BlockSpec((1, tb, D_BRANCH), lambda b, i: (b, i, 0)), state_spec],
        out_shape=[jax.ShapeDtypeStruct((batch, seq, D_BRANCH), F32),
                   jax.ShapeDtypeStruct(s0.shape, F32)],
        scratch_shapes=[pltpu.VMEM((tb + 8, SHIFT_W), F32),
                        pltpu.VMEM((N_HEADS, HEAD_DIM, HEAD_DIM), F32),
                        pltpu.VMEM((tb, D_BRANCH), F32)],
        compiler_params=_params("parallel", "arbitrary"),
        name="rwkv",
    )(sh, gb, s0, prev0, w["mu"], w["w0"], w["a0"], w["lora"], w["kk"], w["ka"],
      w["rk"], w["gn_g"], w["gn_b"])


def _mix_kernel(x_ref, oa_ref, ga_ref, ob_ref, uc_ref, gc_ref, ud_ref, gd_ref, pool0_ref, conv0_ref,
                poolw_ref, pscale_ref, convw_ref, convb_ref, lng_ref, lnb_ref, pw_ref, wout_ref,
                xo_ref, pool1_ref, conv1_ref, pext_ref, cext_ref, *, tm, t_last, pos0):
    i = pl.program_id(1)
    pad_p = POOL_BUF + 1
    pad_c = CONV_BUF + 2

    @pl.when(i == 0)
    def _init():
        pext_ref[0:pad_p, :] = jnp.zeros((pad_p, D_BRANCH), F32)
        cext_ref[0:pad_c, :] = jnp.zeros((pad_c, D_BRANCH), F32)
        pext_ref[1:pad_p, :] = pool0_ref[0]
        cext_ref[2:pad_c, :] = conv0_ref[0]

    @pl.when(i > 0)
    def _carry():
        pext_ref[1:pad_p, :] = pext_ref[tm + 1:tm + pad_p, :]
        cext_ref[2:pad_c, :] = cext_ref[tm + 2:tm + pad_c, :]

    uc = uc_ref[0]
    pext_ref[pad_p:pad_p + tm, :] = uc
    ud = ud_ref[0]
    cext_ref[pad_c:pad_c + tm, :] = ud[:, 0:D_BRANCH] * _sigmoid(ud[:, D_BRANCH:2 * D_BRANCH])

    def back(d):
        return pext_ref[pad_p - d:pad_p - d + tm, :]

    win2 = uc + back(1)
    win4 = win2 + back(2) + back(3)
    win8 = win4
    for d in range(4, 8):
        win8 = win8 + back(d)
    win16 = win8
    for d in range(8, 16):
        win16 = win16 + back(d)
    group = _iota((tm, D_BRANCH), 1) // HEAD_DIM
    win = jnp.where(group == 0, win2, jnp.where(group == 1, win4, jnp.where(group == 2, win8, win16)))
    width = jnp.where(group == 0, 2, jnp.where(group == 1, 4, jnp.where(group == 2, 8, 16)))
    pos = pos0 + i * tm + _iota((tm, D_BRANCH), 0)
    cnt = jnp.minimum(pos + 1, width).astype(F32)
    pooled = win / cnt - uc
    o_c = jnp.dot(pooled.astype(BF16), poolw_ref[...], preferred_element_type=F32) * pscale_ref[...]

    y = jnp.zeros((tm, D_BRANCH), F32) + convb_ref[...]
    for jt in range(CONV_WIDTH):
        y = y + cext_ref[2 + jt:2 + jt + tm, :] * convw_ref[jt:jt + 1, :]
    mean = jnp.mean(y, axis=-1, keepdims=True)
    yc = y - mean
    var = jnp.mean(yc * yc, axis=-1, keepdims=True)
    yn = yc * lax.rsqrt(var + LN_EPS) * lng_ref[...] + lnb_ref[...]
    o_d = jnp.dot(_silu(yn).astype(BF16), pw_ref[...], preferred_element_type=F32)

    acc = jnp.dot((oa_ref[0] * _silu(ga_ref[0])).astype(BF16), wout_ref[0:256, :], preferred_element_type=F32)
    acc = acc + jnp.dot(ob_ref[0].astype(BF16), wout_ref[256:512, :], preferred_element_type=F32)
    acc = acc + jnp.dot((o_c * _silu(gc_ref[0])).astype(BF16), wout_ref[512:768, :], preferred_element_type=F32)
    acc = acc + jnp.dot((o_d * _silu(gd_ref[0])).astype(BF16), wout_ref[768:1024, :], preferred_element_type=F32)
    xo_ref[0] = x_ref[0] + acc

    pool1_ref[0] = pext_ref[t_last + 1:t_last + pad_p, :]
    conv1_ref[0] = cext_ref[t_last + 2:t_last + pad_c, :]


def _mix(x, oa, ga, ob, uc, gc, ud, gd, pool0, conv0, w, batch, seq, tm, t_last, pos0):
    def tok(t, width):
        return t.reshape(batch, seq, width)

    def tok_spec(width):
        return pl.BlockSpec((1, tm, width), lambda b, i: (b, i, 0))

    def full(shape):
        return pl.BlockSpec(shape, lambda b, i: (0,) * len(shape))

    pool_spec = pl.BlockSpec((1, POOL_BUF, D_BRANCH), lambda b, i: (b, 0, 0))
    conv_spec = pl.BlockSpec((1, CONV_BUF, D_BRANCH), lambda b, i: (b, 0, 0))
    kern = functools.partial(_mix_kernel, tm=tm, t_last=t_last, pos0=pos0)
    return pl.pallas_call(
        kern,
        grid=(batch, seq // tm),
        in_specs=[tok_spec(D_MODEL), tok_spec(256), tok_spec(256), tok_spec(256), tok_spec(256), tok_spec(256),
                  tok_spec(512), tok_spec(256), pool_spec, conv_spec,
                  full((D_BRANCH, D_BRANCH)), full((1, D_BRANCH)), full((CONV_WIDTH, D_BRANCH)),
                  full((1, D_BRANCH)), full((1, D_BRANCH)), full((1, D_BRANCH)),
                  full((D_BRANCH, D_BRANCH)), full((D_MODEL, D_MODEL))],
        out_specs=[tok_spec(D_MODEL), pool_spec, conv_spec],
        out_shape=[jax.ShapeDtypeStruct((batch, seq, D_MODEL), F32),
                   jax.ShapeDtypeStruct((batch, POOL_BUF, D_BRANCH), F32),
                   jax.ShapeDtypeStruct((batch, CONV_BUF, D_BRANCH), F32)],
        scratch_shapes=[pltpu.VMEM((POOL_BUF + 1 + tm, D_BRANCH), F32),
                        pltpu.VMEM((CONV_BUF + 2 + tm, D_BRANCH), F32)],
        compiler_params=_params("parallel", "arbitrary"),
        name="mix",
    )(tok(x, D_MODEL), tok(oa, 256), tok(ga, 256), tok(ob, 256), tok(uc, 256), tok(gc, 256), tok(ud, 512),
      tok(gd, 256), pool0, conv0, w["pool_w"], w["pool_scale"], w["conv_w"], w["conv_b"], w["ln_g"], w["ln_b"],
      w["pw_out"], w["w_out"])


def _final_norm_kernel(x_ref, g_ref, o_ref):
    x = x_ref[...]
    ms = jnp.mean(x * x, axis=-1, keepdims=True)
    o_ref[...] = (x * lax.rsqrt(ms + RMS_EPS)) * g_ref[...]


def _final_norm(x, g, tm):
    n = x.shape[0]
    return pl.pallas_call(
        _final_norm_kernel,
        grid=(n // tm,),
        in_specs=[pl.BlockSpec((tm, D_MODEL), lambda i: (i, 0)), pl.BlockSpec((1, D_MODEL), lambda i: (0, 0))],
        out_specs=pl.BlockSpec((tm, D_MODEL), lambda i: (i, 0)),
        out_shape=jax.ShapeDtypeStruct((n, D_MODEL), F32),
        compiler_params=_params("parallel"),
        name="final_norm",
    )(x, g)


def _layer_weights(l, norm_g, w_in, fox_bf, rw_mu, rw_w0, rw_wup, rw_a0, rw_aup, rw_kk, rw_ka, rw_rk,
                   rw_gn_g, rw_gn_b, pool_w, pool_scale, conv_w, conv_b, ln_g, ln_b, pw_out, w_out):
    wi = w_in[l]
    o_f = 3 * D_BRANCH
    o_ga = o_f + N_HEADS
    main = jnp.concatenate([wi[:, 0:D_BRANCH], wi[:, o_ga:]], axis=1).astype(BF16)
    wkv_t = wi[:, D_BRANCH:o_f].T.astype(BF16)
    wf = jnp.zeros((8, D_MODEL), F32).at[0:N_HEADS].set(wi[:, o_f:o_ga].T).astype(BF16)
    bf = jnp.zeros((8, 1), F32).at[0:N_HEADS, 0].set(fox_bf[l])
    lora = jnp.zeros((2 * LORA, 2 * D_BRANCH), F32)
    lora = lora.at[0:LORA, 0:D_BRANCH].set(rw_wup[l]).at[LORA:, D_BRANCH:].set(rw_aup[l]).astype(BF16)
    pw_bd = jnp.zeros((D_BRANCH, D_BRANCH), F32)
    for g in range(len(POOL_WINDOWS)):
        gs = slice(g * HEAD_DIM, (g + 1) * HEAD_DIM)
        pw_bd = pw_bd.at[gs, gs].set(pool_w[l, g])

    def row(t):
        return t[l].reshape(1, -1)

    return dict(
        norm_g=row(norm_g), main=main, wkv_t=wkv_t, wf=wf, bf=bf,
        rwkv=dict(mu=row(rw_mu), w0=row(rw_w0), a0=row(rw_a0), lora=lora, kk=row(rw_kk), ka=row(rw_ka),
                  rk=row(rw_rk), gn_g=row(rw_gn_g), gn_b=row(rw_gn_b)),
        mix=dict(pool_w=pw_bd.astype(BF16), pool_scale=row(pool_scale), conv_w=conv_w[l], conv_b=row(conv_b),
                 ln_g=row(ln_g), ln_b=row(ln_b), pw_out=pw_out[l].astype(BF16), w_out=w_out[l].astype(BF16)))


def _stream_layer(x, lw, attend, wkv0, shift0, pool0, conv0, tiles, t_valid, pos0, flat_inproj):
    batch, seq, _ = x.shape
    x_in = x.reshape(1, batch * seq, D_MODEL) if flat_inproj else x
    *tok, kt, vt, lf_t = _inproj(x_in, lw["norm_g"], lw["main"], lw["wkv_t"], lw["wf"], lw["bf"], tiles["inproj"])
    q, ga, sh, gb, uc, gc, ud, gd = (t.reshape(batch, seq, t.shape[-1]) for t in tok)
    oa = attend(q, kt, vt, lf_t)
    if flat_inproj:
        prev0 = jnp.pad(shift0[:, None, :], ((0, 0), (0, seq - 1), (0, 0))).reshape(1, batch * seq, SHIFT_W)
        ob, wkv1 = _rwkv(sh.reshape(1, batch * seq, SHIFT_W), gb.reshape(1, batch * seq, D_BRANCH), wkv0, prev0,
                         lw["rwkv"], tiles["rwkv"], seq, t_valid, False)
        ob = ob.reshape(batch, seq, D_BRANCH)
    else:
        ob, wkv1 = _rwkv(sh, gb, wkv0, shift0[:, None, :], lw["rwkv"], tiles["rwkv"], tiles["chunk"],
                         tiles["chunk"], True)
    t_last = t_valid - (seq - tiles["mix"])
    x1, pool1, conv1 = _mix(x, oa, ga, ob, uc, gc, ud, gd, pool0, conv0, lw["mix"], batch, seq, tiles["mix"],
                            t_last, pos0)
    return x1, kt, vt, sh, lf_t, wkv1, pool1, conv1


def kernel(x_prompt, x_sample, cache_k, cache_v, cache_logf, state_wkv, state_shift, state_pool, state_conv,
           page_table, norm_g, w_in, fox_bf, rw_mu, rw_w0, rw_wup, rw_a0, rw_aup, rw_kk, rw_ka, rw_rk,
           rw_gn_g, rw_gn_b, pool_w, pool_scale, conv_w, conv_b, ln_g, ln_b, pw_out, w_out, final_norm_g):
    bp, tp, _ = x_prompt.shape
    bs, ts, _ = x_sample.shape
    depth = w_in.shape[0]
    n_phys = cache_k.shape[0]
    past = page_table.shape[1] * PAGE
    weights = (norm_g, w_in, fox_bf, rw_mu, rw_w0, rw_wup, rw_a0, rw_aup, rw_kk, rw_ka, rw_rk, rw_gn_g, rw_gn_b,
               pool_w, pool_scale, conv_w, conv_b, ln_g, ln_b, pw_out, w_out)

    tile_p = min(256, tp)
    tiles_p = dict(inproj=tile_p, rwkv=tile_p, chunk=min(64, tile_p), mix=tile_p)
    tiles_s = dict(inproj=min(256, bs * SAMPLE_PAD), rwkv=min(64, bs * SAMPLE_PAD), chunk=SAMPLE_PAD, mix=SAMPLE_PAD)

    xp = x_prompt
    xs = jnp.pad(x_sample, ((0, 0), (0, SAMPLE_PAD - ts), (0, 0)))
    cache_kt = jnp.transpose(cache_k, (0, 1, 3, 4, 2))
    cache_vt = jnp.transpose(cache_v, (0, 1, 3, 4, 2))
    cache_lft = jnp.transpose(cache_logf, (0, 1, 3, 2))

    wkv_zero = jnp.zeros((bp, N_HEADS, HEAD_DIM, HEAD_DIM), F32)
    shift_zero = jnp.zeros((bp, SHIFT_W), F32)
    pool_zero = jnp.zeros((bp, POOL_BUF, D_BRANCH), F32)
    conv_zero = jnp.zeros((bp, CONV_BUF, D_BRANCH), F32)

    def sample_heads(t):
        return t.reshape(N_HEADS, HEAD_DIM, bs, SAMPLE_PAD).transpose(2, 0, 1, 3)

    def pad_keys(t):
        return jnp.pad(t, [(0, 0)] * (t.ndim - 1) + [(0, PAGE - SAMPLE_PAD)])

    outs_p, outs_s = [], []
    for l in range(depth):
        lw = _layer_weights(l, *weights)

        attend_p = lambda q, kt, vt, lf_t: _fox_prompt(q, kt, vt, lf_t, tile_p)
        xp, kt_p, vt_p, sh_p, lf_p, wkv_p, pool_p, conv_p = _stream_layer(
            xp, lw, attend_p, wkv_zero, shift_zero, pool_zero, conv_zero, tiles_p, tp, 0, False)
        outs_p.append((kt_p.reshape(bp, N_HEADS, HEAD_DIM, tp), vt_p.reshape(bp, N_HEADS, HEAD_DIM, tp),
                       lf_p[:, 0:N_HEADS], wkv_p, sh_p[:, tp - 1], pool_p, conv_p))

        def attend_s(q, kt, vt, lf_t, l=l):
            lf_new = lf_t[0, 0:N_HEADS].reshape(N_HEADS, bs, SAMPLE_PAD).transpose(1, 0, 2)
            return _fox_sample(q, pad_keys(sample_heads(kt)), pad_keys(sample_heads(vt)), pad_keys(lf_new),
                               cache_kt, cache_vt, cache_lft, page_table, l, ts)

        xs, kt_s, vt_s, sh_s, lf_s, wkv_s, pool_s, conv_s = _stream_layer(
            xs, lw, attend_s, state_wkv[:, l], state_shift[:, l], state_pool[:, l], state_conv[:, l],
            tiles_s, ts, past, True)
        outs_s.append((sample_heads(kt_s).transpose(0, 3, 1, 2)[:, 0:ts],
                       sample_heads(vt_s).transpose(0, 3, 1, 2)[:, 0:ts],
                       lf_s[0, 0:N_HEADS].reshape(N_HEADS, bs, SAMPLE_PAD).transpose(1, 2, 0)[:, 0:ts],
                       wkv_s, sh_s[:, ts - 1], pool_s, conv_s))

    fg = final_norm_g.reshape(1, D_MODEL)
    y_prompt = _final_norm(xp.reshape(bp * tp, D_MODEL), fg, tile_p).reshape(bp, tp, D_MODEL)
    y_sample = _final_norm(xs.reshape(bs * SAMPLE_PAD, D_MODEL), fg, min(256, bs * SAMPLE_PAD))
    y_sample = y_sample.reshape(bs, SAMPLE_PAD, D_MODEL)[:, 0:ts]

    def stk(outs, i):
        return jnp.stack([o[i] for o in outs], axis=1)

    k_prompt = stk(outs_p, 0).transpose(0, 1, 4, 2, 3)
    v_prompt = stk(outs_p, 1).transpose(0, 1, 4, 2, 3)
    logf_prompt = stk(outs_p, 2).transpose(0, 1, 3, 2)
    return (y_prompt, y_sample, k_prompt, v_prompt, logf_prompt,
            *(stk(outs_p, i) for i in range(3, 7)),
            *(stk(outs_s, i) for i in range(7)))
```

```python
import functools

import jax
import jax.numpy as jnp
from jax import lax
from jax.experimental import pallas as pl
from jax.experimental.pallas import tpu as pltpu

F32 = jnp.float32
BF16 = jnp.bfloat16

D_MODEL = 1024
D_BRANCH = 256
HEAD_DIM = 64
N_HEADS = 4
LORA = 64
SHIFT_W = 3 * D_BRANCH + 2 * LORA
PAGE = 128
POOL_WINDOWS = (2, 4, 8, 16)
POOL_BUF = 15
CONV_WIDTH = 31
CONV_BUF = 30
RMS_EPS = 1e-6
LN_EPS = 1e-5
GN_EPS = 64e-5
ATTN_SCALE = HEAD_DIM ** -0.5
NEG_BIG = -1e30
SAMPLE_PAD = 8
PAGES_PER_STEP = 8
RWKV_PIECE = 64
VMEM_LIMIT = 56 * 1024 * 1024

NN = ((1,), (0,))
NT = ((1,), (1,))
TN = ((0,), (0,))


def _dg(a, b, dims=NN):
    return lax.dot_general(a, b, (dims, ((), ())), preferred_element_type=F32)


def _split3(x):
    hi = x.astype(BF16)
    r1 = x - hi.astype(F32)
    mid = r1.astype(BF16)
    lo = (r1 - mid.astype(F32)).astype(BF16)
    return hi, mid, lo


def _dot_f32(a, b, dims=NN):
    ah, am, _ = _split3(a)
    bh, bm, _ = _split3(b)
    return _dg(ah, bh, dims) + (_dg(ah, bm, dims) + _dg(am, bh, dims))


def _dot_exact_rhs(a, b_bf16, dims=NN):
    ah, am, al = _split3(a)
    return _dg(ah, b_bf16, dims) + (_dg(am, b_bf16, dims) + _dg(al, b_bf16, dims))


def _dot_exact_lhs(a_bf16, b, dims=NN):
    bh, bm, bl = _split3(b)
    return _dg(a_bf16, bh, dims) + (_dg(a_bf16, bm, dims) + _dg(a_bf16, bl, dims))


def _group_sum(a, ones_bf16):
    hi = a.astype(BF16)
    mid = (a - hi.astype(F32)).astype(BF16)
    return _dg(hi, ones_bf16) + _dg(mid, ones_bf16)


def _sigmoid(x):
    return 1.0 / (1.0 + jnp.exp(-x))


def _silu(x):
    return x * _sigmoid(x)


def _softplus(x):
    return jnp.maximum(x, 0.0) + jnp.log1p(jnp.exp(-jnp.abs(x)))


def _iota(shape, dim):
    return lax.broadcasted_iota(jnp.int32, shape, dim)


def _params(*sem):
    return pltpu.CompilerParams(dimension_semantics=sem, vmem_limit_bytes=VMEM_LIMIT)


_IN_SEGS = (("q", 256), ("ga", 256), ("sh", SHIFT_W), ("gb", 256),
            ("uc", 256), ("gc", 256), ("ud", 512), ("gd", 256))
_IN_COLS = sum(s for _, s in _IN_SEGS)


def _inproj_kernel(x_ref, g_ref, w_ref, wkv_ref, wf_ref, bf_ref, *out_refs):
    x = x_ref[0]
    ms = jnp.mean(x * x, axis=-1, keepdims=True)
    h = ((x * lax.rsqrt(ms + RMS_EPS)) * g_ref[...]).astype(BF16)
    off = 0
    for (_, size), o_ref in zip(_IN_SEGS, out_refs[:-3]):
        o_ref[0] = jnp.dot(h, w_ref[:, off:off + size], preferred_element_type=F32)
        off += size
    kt_ref, vt_ref, lf_ref = out_refs[-3:]
    kt_ref[0] = _dg(wkv_ref[0:D_BRANCH, :], h, NT)
    vt_ref[0] = _dg(wkv_ref[D_BRANCH:2 * D_BRANCH, :], h, NT)
    f = _dg(wf_ref[...], h, NT) + bf_ref[...]
    lf_ref[0] = -_softplus(-f)


def _inproj(x, g, w, wkv_t, wf, bf, tm):
    batch, seq, _ = x.shape
    out_shape = [jax.ShapeDtypeStruct((batch, seq, s), F32) for _, s in _IN_SEGS]
    out_shape += [jax.ShapeDtypeStruct((batch, D_BRANCH, seq), F32)] * 2
    out_shape.append(jax.ShapeDtypeStruct((batch, 8, seq), F32))
    out_specs = [pl.BlockSpec((1, tm, s), lambda b, i: (b, i, 0)) for _, s in _IN_SEGS]
    out_specs += [pl.BlockSpec((1, D_BRANCH, tm), lambda b, i: (b, 0, i))] * 2
    out_specs.append(pl.BlockSpec((1, 8, tm), lambda b, i: (b, 0, i)))

    def full(shape):
        return pl.BlockSpec(shape, lambda b, i: (0,) * len(shape))

    return pl.pallas_call(
        _inproj_kernel,
        grid=(batch, seq // tm),
        in_specs=[pl.BlockSpec((1, tm, D_MODEL), lambda b, i: (b, i, 0)),
                  full((1, D_MODEL)), full((D_MODEL, _IN_COLS)), full((2 * D_BRANCH, D_MODEL)),
                  full((8, D_MODEL)), full((8, 1))],
        out_specs=out_specs,
        out_shape=out_shape,
        compiler_params=_params("parallel", "parallel"),
        name="inproj",
    )(x, g, w, wkv_t, wf, bf)


def _fox_prompt_kernel(q_ref, kt_ref, vt_ref, lf_ref, o_ref, c_ref, kb_ref, vb_ref, m_ref, acc_ref, *, tq, seq):
    i = pl.program_id(1)
    heads = [slice(h * HEAD_DIM, (h + 1) * HEAD_DIM) for h in range(N_HEADS)]

    @pl.when(i == 0)
    def _per_sequence():
        upper = (_iota((128, 128), 0) <= _iota((128, 128), 1)).astype(BF16)
        carry = jnp.zeros((8, 1), F32)
        for blk in range(seq // 128):
            cs = _dot_exact_rhs(lf_ref[0, :, blk * 128:(blk + 1) * 128], upper) + carry
            c_ref[:, blk * 128:(blk + 1) * 128] = cs
            carry = cs[:, 127:128]
        kb_ref[...] = kt_ref[0].astype(BF16)
        for h, hs in enumerate(heads):
            vb_ref[h, 0:HEAD_DIM, :] = vt_ref[0, hs, :].astype(BF16)
            vb_ref[h, HEAD_DIM:2 * HEAD_DIM, :] = jnp.ones((HEAD_DIM, seq), BF16)

    row0 = pl.multiple_of(i * tq, tq)
    c_tile = c_ref[:, pl.ds(row0, tq)]
    causal = _iota((tq, tq), 1) <= _iota((tq, tq), 0)

    q_all = q_ref[0] * ATTN_SCALE
    q_heads = [q_all[:, hs].astype(BF16) for hs in heads]
    m_ref[...] = jnp.full(m_ref.shape, NEG_BIG, F32)
    acc_ref[...] = jnp.zeros(acc_ref.shape, F32)

    def tile_update(col0, masked):
        for h, hs in enumerate(heads):
            kh = kb_ref[hs, pl.ds(col0, tq)]
            vh = vb_ref[h, :, pl.ds(col0, tq)]
            s = _dg(q_heads[h], kh) + (c_tile[h:h + 1, 0:1] - c_ref[h:h + 1, pl.ds(col0, tq)])
            if masked:
                s = jnp.where(causal, s, NEG_BIG)
            m_old = m_ref[h]
            m_new = jnp.maximum(m_old, jnp.max(s, axis=-1, keepdims=True))
            p = jnp.exp(s - jnp.tile(m_new, (1, tq // 128)))
            alpha = jnp.exp(m_old - m_new)
            acc_ref[h] = alpha * acc_ref[h] + _dg(p.astype(BF16), vh, NT)
            m_ref[h] = m_new

    def body(j, carry):
        tile_update(pl.multiple_of(j * tq, tq), False)
        return carry

    lax.fori_loop(0, i, body, 0)
    tile_update(row0, True)
    for h, hs in enumerate(heads):
        a = acc_ref[h]
        o_ref[0, :, hs] = a[:, 0:HEAD_DIM] / a[:, HEAD_DIM:2 * HEAD_DIM]


def _fox_prompt(q, kt, vt, lf_t, tq):
    batch, seq, _ = q.shape
    kern = functools.partial(_fox_prompt_kernel, tq=tq, seq=seq)
    return pl.pallas_call(
        kern,
        grid=(batch, seq // tq),
        in_specs=[pl.BlockSpec((1, tq, D_BRANCH), lambda b, i: (b, i, 0)),
                  pl.BlockSpec((1, D_BRANCH, seq), lambda b, i: (b, 0, 0)),
                  pl.BlockSpec((1, D_BRANCH, seq), lambda b, i: (b, 0, 0)),
                  pl.BlockSpec((1, 8, seq), lambda b, i: (b, 0, 0))],
        out_specs=pl.BlockSpec((1, tq, D_BRANCH), lambda b, i: (b, i, 0)),
        out_shape=jax.ShapeDtypeStruct((batch, seq, D_BRANCH), F32),
        scratch_shapes=[pltpu.VMEM((8, seq), F32),
                        pltpu.VMEM((D_BRANCH, seq), BF16), pltpu.VMEM((N_HEADS, 2 * HEAD_DIM, seq), BF16),
                        pltpu.VMEM((N_HEADS, tq, 128), F32), pltpu.VMEM((N_HEADS, tq, 2 * HEAD_DIM), F32)],
        compiler_params=_params("parallel", "arbitrary"),
        name="fox_prompt",
    )(q, kt, vt, lf_t)


def _fox_sample_kernel(pt_ref, q_ref, kn_ref, vn_ref, lfn_ref, ck_ref, cv_ref, clf_ref, o_ref,
                       kbuf, vbuf, lfbuf, suffix_ref, total_ref, s_ref, ksem, vsem, lfsem,
                       *, layer, n_pages, t_valid):
    b = pl.program_id(0)
    slot = b % 2
    rows = N_HEADS * SAMPLE_PAD
    group = PAGES_PER_STEP

    def page_copies(seq, slot_, p):
        page = pt_ref[seq, p]
        return (pltpu.make_async_copy(ck_ref.at[page, layer], kbuf.at[slot_, p], ksem.at[slot_]),
                pltpu.make_async_copy(cv_ref.at[page, layer], vbuf.at[slot_, p], vsem.at[slot_]),
                pltpu.make_async_copy(clf_ref.at[page, layer], lfbuf.at[slot_, p, 0:N_HEADS], lfsem.at[slot_]))

    def start_fetch(seq, slot_):
        def body(p, carry):
            for cp in page_copies(seq, slot_, p):
                cp.start()
            return carry
        lax.fori_loop(0, n_pages, body, 0)

    def wait_fetch(seq, slot_):
        def body(p, carry):
            for cp in page_copies(seq, slot_, p):
                cp.wait()
            return carry
        lax.fori_loop(0, n_pages, body, 0)

    @pl.when(b == 0)
    def _first():
        lfbuf[...] = jnp.zeros(lfbuf.shape, F32)
        start_fetch(0, 0)

    @pl.when(b + 1 < pl.num_programs(0))
    def _prefetch_next():
        start_fetch(b + 1, 1 - slot)

    wait_fetch(b, slot)

    q_all = q_ref[0] * ATTN_SCALE
    q_heads = [q_all[:, h * HEAD_DIM:(h + 1) * HEAD_DIM].astype(BF16) for h in range(N_HEADS)]

    key4 = _iota((N_HEADS, PAGE), 1)
    lfbuf[slot, n_pages, 0:N_HEADS, :] = jnp.where(key4 < t_valid, lfn_ref[0], 0.0)
    lf_all = lfbuf[slot].reshape((n_pages + 1) * 8, PAGE)
    later = (_iota((PAGE, PAGE), 0) > _iota((PAGE, PAGE), 1)).astype(BF16)
    suffix_ref[...] = _dot_exact_rhs(lf_all, later)
    total_ref[...] = _dot_exact_rhs(lf_all, jnp.ones((PAGE, PAGE), BF16))

    def scores(kt, bias8, col0, mask):
        for h in range(N_HEADS):
            s = _dg(q_heads[h], kt[h].astype(BF16)) + bias8[h:h + 1, :]
            if mask is not None:
                s = jnp.where(mask, s, NEG_BIG)
            s_ref[h * SAMPLE_PAD:(h + 1) * SAMPLE_PAD, pl.ds(col0, PAGE)] = s

    new0 = n_pages * 8
    key = _iota((SAMPLE_PAD, PAGE), 1)
    qry = _iota((SAMPLE_PAD, PAGE), 0)
    scores(kn_ref[0], suffix_ref[new0:new0 + 8, :], n_pages * PAGE,
           jnp.logical_and(key <= qry, key < t_valid))

    def score_group(g, running):
        for i in range(group):
            p = n_pages - 1 - (g * group + i)
            r0 = pl.multiple_of(p * 8, 8)
            scores(kbuf[slot, p], suffix_ref[pl.ds(r0, 8), :] + running, pl.multiple_of(p * PAGE, PAGE), None)
            running = running + total_ref[pl.ds(r0, 8), :]
        return running

    lax.fori_loop(0, n_pages // group, score_group, total_ref[new0:new0 + 8, :])

    s = s_ref[...]
    m = jnp.max(s, axis=-1, keepdims=True)
    p_all = jnp.exp(s - m)
    denom = jnp.sum(p_all, axis=-1, keepdims=True)
    s_ref[...] = p_all

    def weighted(vt, col0):
        return jnp.concatenate(
            [_dg(s_ref[h * SAMPLE_PAD:(h + 1) * SAMPLE_PAD, pl.ds(col0, PAGE)].astype(BF16),
                 vt[h].astype(BF16), NT) for h in range(N_HEADS)], axis=0)

    def value_group(g, acc):
        for i in range(group):
            p = g * group + i
            acc = acc + weighted(vbuf[slot, p], pl.multiple_of(p * PAGE, PAGE))
        return acc

    acc = lax.fori_loop(0, n_pages // group, value_group, weighted(vn_ref[0], n_pages * PAGE))
    out = acc / denom
    for h in range(N_HEADS):
        o_ref[0, :, h * HEAD_DIM:(h + 1) * HEAD_DIM] = out[h * SAMPLE_PAD:(h + 1) * SAMPLE_PAD]


def _fox_sample(q, kt_new, vt_new, lf_new, cache_kt, cache_vt, cache_lft, page_table, layer, t_valid):
    batch, n_pages = page_table.shape
    rows = N_HEADS * SAMPLE_PAD

    def per_seq(shape):
        return pl.BlockSpec((1,) + shape, lambda b, pt: (b,) + (0,) * len(shape))

    in_specs = [per_seq((SAMPLE_PAD, D_BRANCH)), per_seq((N_HEADS, HEAD_DIM, PAGE)),
                per_seq((N_HEADS, HEAD_DIM, PAGE)), per_seq((N_HEADS, PAGE)),
                pl.BlockSpec(memory_space=pl.ANY), pl.BlockSpec(memory_space=pl.ANY),
                pl.BlockSpec(memory_space=pl.ANY)]
    kern = functools.partial(_fox_sample_kernel, layer=layer, n_pages=n_pages, t_valid=t_valid)
    return pl.pallas_call(
        kern,
        grid_spec=pltpu.PrefetchScalarGridSpec(
            num_scalar_prefetch=1,
            grid=(batch,),
            in_specs=in_specs,
            out_specs=per_seq((SAMPLE_PAD, D_BRANCH)),
            scratch_shapes=[pltpu.VMEM((2, n_pages, N_HEADS, HEAD_DIM, PAGE), F32),
                            pltpu.VMEM((2, n_pages, N_HEADS, HEAD_DIM, PAGE), F32),
                            pltpu.VMEM((2, n_pages + 1, 8, PAGE), F32),
                            pltpu.VMEM(((n_pages + 1) * 8, PAGE), F32),
                            pltpu.VMEM(((n_pages + 1) * 8, PAGE), F32),
                            pltpu.VMEM((rows, (n_pages + 1) * PAGE), F32),
                            pltpu.SemaphoreType.DMA((2,)), pltpu.SemaphoreType.DMA((2,)),
                            pltpu.SemaphoreType.DMA((2,))]),
        out_shape=jax.ShapeDtypeStruct((batch, SAMPLE_PAD, D_BRANCH), F32),
        compiler_params=_params("arbitrary"),
        name="fox_sample",
    )(page_table, q, kt_new, vt_new, lf_new, cache_kt, cache_vt, cache_lft)


def _unit_lower_inverses(ns, size, chunk):
    r = _iota((size, size), 0)
    c = _iota((size, size), 1)
    eye = (r == c).astype(F32)
    pair = (r // 2) == (c // 2)
    ms = [eye - jnp.where(pair, n.astype(F32), 0.0) for n in ns]
    blk = 2
    while blk < chunk:
        off = jnp.logical_and((r // (2 * blk)) == (c // (2 * blk)), (r // blk) != (c // blk))
        mbs = [m.astype(BF16) for m in ms]
        ts = [_dg(mb, n).astype(BF16) for mb, n in zip(mbs, ns)]
        ts = [_dg(t, mb) for t, mb in zip(ts, mbs)]
        ms = [m - jnp.where(off, t, 0.0) for m, t in zip(ms, ts)]
        blk *= 2
    return ms


def _rwkv_kernel(sh_ref, gb_ref, s0_ref, prev0_ref, mu_ref, w0_ref, a0_ref, lora_ref, kkw_ref, ka_ref,
                 rk_ref, gng_ref, gnb_ref, o_ref, s1_ref, ext_ref, st_ref, oacc_ref, *, tb, chunk, t_valid,
                 carried):
    n_blk = tb // chunk
    sh = sh_ref[0]
    if carried:
        i = pl.program_id(1)

        @pl.when(i == 0)
        def _init():
            st_ref[...] = s0_ref[0]
            ext_ref[0:8, :] = jnp.zeros((8, SHIFT_W), F32)
            ext_ref[7:8, :] = prev0_ref[0]

        @pl.when(i > 0)
        def _carry_shift():
            ext_ref[7:8, :] = ext_ref[tb + 7:tb + 8, :]

        ext_ref[8:8 + tb, :] = sh
        prev = ext_ref[7:7 + tb, :]
    else:
        ext_ref[0:8, :] = jnp.zeros((8, SHIFT_W), F32)
        ext_ref[8:8 + tb, :] = sh
        first = (_iota((tb, 1), 0) % chunk) == 0
        prev = jnp.where(first, prev0_ref[0], ext_ref[7:7 + tb, :])
    xm = sh + mu_ref[...] * (prev - sh)
    r = xm[:, 0:256]
    k = xm[:, 256:512]
    v = xm[:, 512:768]
    wa = xm[:, 768:896]
    wa = jnp.where(_iota((tb, 2 * LORA), 1) < LORA, jnp.tanh(wa), wa)
    lora = jnp.dot(wa.astype(BF16), lora_ref[...], preferred_element_type=F32)
    w_log = -_softplus(-(w0_ref[...] + lora[:, 0:256])) - 0.5
    log_decay = -jnp.exp(w_log)
    a = _sigmoid(a0_ref[...] + lora[:, 256:512])

    grp = ((_iota((D_BRANCH, D_BRANCH), 0) // HEAD_DIM) == (_iota((D_BRANCH, D_BRANCH), 1) // HEAD_DIM)).astype(BF16)
    kk = k * kkw_ref[...]
    kk = kk * lax.rsqrt(jnp.maximum(_group_sum(kk * kk, grp), 1e-12))
    kp = k * (1.0 + (a - 1.0) * ka_ref[...])
    beta = kk * a
    bonus = _group_sum(r * kp * rk_ref[...], grp) * v
    if t_valid < chunk:
        keep = ((_iota((tb, 1), 0) % chunk) < t_valid).astype(F32)
        log_decay = log_decay * keep
        kp = kp * keep
        beta = beta * keep

    rr = _iota((tb, tb), 0)
    cc = _iota((tb, tb), 1)
    incl = jnp.logical_and((rr // chunk) == (cc // chunk), rr >= cc)

    lc = _dot_exact_lhs(incl.astype(BF16), log_decay)
    lc_tot = jnp.concatenate(
        [jnp.broadcast_to(lc[(c + 1) * chunk - 1:(c + 1) * chunk, :], (chunk, D_BRANCH)) for c in range(n_blk)],
        axis=0)
    to_end = jnp.exp(lc_tot - lc)
    g_inv = jnp.exp(-lc)
    g_end = jnp.exp(lc_tot)
    qc = kk * jnp.exp(lc - log_decay)
    rc = r * jnp.exp(lc)
    kc = kp * g_inv
    bc = beta * g_inv
    kd = kp * to_end
    bd = beta * to_end
    eye_k = (_iota((HEAD_DIM, HEAD_DIM), 0) == _iota((HEAD_DIM, HEAD_DIM), 1)).astype(F32)

    def b16(t):
        return t.astype(BF16)

    mat = min(tb, RWKV_PIECE)
    n_sub = tb // mat
    rm = _iota((mat, mat), 0)
    cm = _iota((mat, mat), 1)
    same_m = (rm // chunk) == (cm // chunk)
    incl_m = jnp.logical_and(same_m, rm >= cm)
    strict_m = jnp.logical_and(same_m, rm > cm)
    pieces = [(sub, h) for sub in range(n_sub) for h in range(N_HEADS)]

    def piece(t, sub, h):
        return t[sub * mat:(sub + 1) * mat, h * HEAD_DIM:(h + 1) * HEAD_DIM]

    r_f = [piece(rc, *ph) for ph in pieces]
    v_f = [piece(v, *ph) for ph in pieces]
    kd_f = [piece(kd, *ph) for ph in pieces]
    bd_f = [piece(bd, *ph) for ph in pieces]
    q_b = [b16(piece(qc, *ph)) for ph in pieces]
    k_b = [b16(piece(kc, *ph)) for ph in pieces]
    b_b = [b16(piece(bc, *ph)) for ph in pieces]
    r_b = [b16(t) for t in r_f]
    v_b = [b16(t) for t in v_f]
    a_qk = [b16(jnp.where(strict_m, _dg(q, k, NT), 0.0)) for q, k in zip(q_b, k_b)]
    a_qb = [b16(jnp.where(strict_m, _dg(q, bb, NT), 0.0)) for q, bb in zip(q_b, b_b)]
    a_rk = [b16(jnp.where(incl_m, _dg(rr_, k, NT), 0.0)) for rr_, k in zip(r_b, k_b)]
    a_rb = [b16(jnp.where(incl_m, _dg(rr_, bb, NT), 0.0)) for rr_, bb in zip(r_b, b_b)]
    inv = [b16(t) for t in _unit_lower_inverses(a_qb, mat, chunk)]
    w_f = [_dg(m_, q) for m_, q in zip(inv, q_b)]
    av_b = [b16(_dg(a, vv)) for a, vv in zip(a_qk, v_b)]
    y_f = [_dg(m_, av) for m_, av in zip(inv, av_b)]
    o0 = [_dg(ark, vv) - _dg(arb, b16(yy)) for ark, arb, vv, yy in zip(a_rk, a_rb, v_b, y_f)]
    rp = [rr_ - _dg(arb, b16(ww)) for rr_, arb, ww in zip(r_f, a_rb, w_f)]

    blocks_per_sub = mat // chunk
    for h in range(N_HEADS):
        hs = slice(h * HEAD_DIM, (h + 1) * HEAD_DIM)
        state = st_ref[h] if carried else None
        for c in range(n_blk):
            sub, loc = divmod(c, blocks_per_sub)
            idx = sub * N_HEADS + h
            ls = slice(loc * chunk, (loc + 1) * chunk)
            sl = slice(c * chunk, (c + 1) * chunk)
            bd_c = b16(bd_f[idx][ls])
            p_mat = eye_k * g_end[c * chunk:c * chunk + 1, hs] - _dg(b16(w_f[idx][ls]), bd_c, TN)
            z_mat = _dg(v_b[idx][ls] if chunk % 16 == 0 else b16(v_f[idx][ls]), b16(kd_f[idx][ls]), TN) \
                - _dg(b16(y_f[idx][ls]), bd_c, TN)
            s_in = state if carried else s0_ref[c, h]
            oacc_ref[sl, hs] = o0[idx][ls] + _dg(b16(rp[idx][ls]), b16(s_in), NT)
            s_out = _dot_f32(s_in, p_mat) + z_mat
            if carried:
                state = s_out
            else:
                s1_ref[c, h] = s_out
        if carried:
            st_ref[h] = state

    o = oacc_ref[...]
    mean = _group_sum(o, grp) * (1.0 / HEAD_DIM)
    d = o - mean
    var = _group_sum(d * d, grp) * (1.0 / HEAD_DIM)
    o = d * lax.rsqrt(var + GN_EPS) * gng_ref[...] + gnb_ref[...] + bonus
    o_ref[0] = o * _silu(gb_ref[0])
    if carried:
        s1_ref[0] = st_ref[...]


def _rwkv(sh, gb, s0, prev0, w, tb, chunk, t_valid, carried):
    batch, seq, _ = sh.shape
    kern = functools.partial(_rwkv_kernel, tb=tb, chunk=chunk, t_valid=t_valid, carried=carried)

    def row(width):
        return pl.BlockSpec((1, width), lambda b, i: (0, 0))

    if carried:
        state_spec = pl.BlockSpec((1, N_HEADS, HEAD_DIM, HEAD_DIM), lambda b, i: (b, 0, 0, 0))
        prev_spec = pl.BlockSpec((1, 1, SHIFT_W), lambda b, i: (b, 0, 0))
    else:
        state_spec = pl.BlockSpec((tb // chunk, N_HEADS, HEAD_DIM, HEAD_DIM), lambda b, i: (i, 0, 0, 0))
        prev_spec = pl.BlockSpec((1, tb, SHIFT_W), lambda b, i: (b, i, 0))
    return pl.pallas_call(
        kern,
        grid=(batch, seq // tb),
        in_specs=[pl.BlockSpec((1, tb, SHIFT_W), lambda b, i: (b, i, 0)),
                  pl.BlockSpec((1, tb, D_BRANCH), lambda b, i: (b, i, 0)),
                  state_spec, prev_spec,
                  row(SHIFT_W), row(D_BRANCH), row(D_BRANCH),
                  pl.BlockSpec((2 * LORA, 2 * D_BRANCH), lambda b, i: (0, 0)),
                  row(D_BRANCH), row(D_BRANCH), row(D_BRANCH), row(D_BRANCH), row(D_BRANCH)],
        out_specs=[pl.BlockSpec((1, tb, D_BRANCH), lambda b, i: (b, i, 0)), state_spec],
        out_shape=[jax.ShapeDtypeStruct((batch, seq, D_BRANCH), F32),
                   jax.ShapeDtypeStruct(s0.shape, F32)],
        scratch_shapes=[pltpu.VMEM((tb + 8, SHIFT_W), F32),
                        pltpu.VMEM((N_HEADS, HEAD_DIM, HEAD_DIM), F32),
                        pltpu.VMEM((tb, D_BRANCH), F32)],
        compiler_params=_params("parallel", "arbitrary"),
        name="rwkv",
    )(sh, gb, s0, prev0, w["mu"], w["w0"], w["a0"], w["lora"], w["kk"], w["ka"],
      w["rk"], w["gn_g"], w["gn_b"])


def _mix_kernel(x_ref, oa_ref, ga_ref, ob_ref, uc_ref, gc_ref, ud_ref, gd_ref, pool0_ref, conv0_ref,
                poolw_ref, pscale_ref, convw_ref, convb_ref, lng_ref, lnb_ref, pw_ref, wout_ref,
                xo_ref, pool1_ref, conv1_ref, pext_ref, cext_ref, pshift_ref, cshift_ref, *, tm, t_last, pos0):
    i = pl.program_id(1)
    pad_p = POOL_BUF + 1
    pad_c = CONV_BUF + 2

    @pl.when(i == 0)
    def _init():
        pext_ref[0:pad_p, :] = jnp.zeros((pad_p, D_BRANCH), F32)
        cext_ref[0:pad_c, :] = jnp.zeros((pad_c, D_BRANCH), F32)
        pext_ref[1:pad_p, :] = pool0_ref[0]
        cext_ref[2:pad_c, :] = conv0_ref[0]

    @pl.when(i > 0)
    def _carry():
        pext_ref[1:pad_p, :] = pext_ref[tm + 1:tm + pad_p, :]
        cext_ref[2:pad_c, :] = cext_ref[tm + 2:tm + pad_c, :]

    uc = uc_ref[0]
    pext_ref[pad_p:pad_p + tm, :] = uc
    ud = ud_ref[0]
    cext_ref[pad_c:pad_c + tm, :] = ud[:, 0:D_BRANCH] * _sigmoid(ud[:, D_BRANCH:2 * D_BRANCH])

    done_p = set()

    def back(d):
        a, sub = divmod(pad_p - d, 8)
        if sub not in done_p:
            done_p.add(sub)
            pshift_ref[sub] = pext_ref[sub:sub + tm + 8, :]
        return pshift_ref[sub, 8 * a:8 * a + tm, :]

    win2 = uc + back(1)
    win4 = win2 + back(2) + back(3)
    win8 = win4
    for d in range(4, 8):
        win8 = win8 + back(d)
    win16 = win8
    for d in range(8, 16):
        win16 = win16 + back(d)
    group = _iota((tm, D_BRANCH), 1) // HEAD_DIM
    win = jnp.where(group == 0, win2, jnp.where(group == 1, win4, jnp.where(group == 2, win8, win16)))
    width = jnp.where(group == 0, 2, jnp.where(group == 1, 4, jnp.where(group == 2, 8, 16)))
    pos = pos0 + i * tm + _iota((tm, D_BRANCH), 0)
    cnt = jnp.minimum(pos + 1, width).astype(F32)
    pooled = win / cnt - uc
    o_c = jnp.dot(pooled.astype(BF16), poolw_ref[...], preferred_element_type=F32) * pscale_ref[...]

    y = jnp.zeros((tm, D_BRANCH), F32) + convb_ref[...]
    for sub in range(8):
        taps = range(sub, CONV_WIDTH, 8)
        span = tm + 8 * (len(taps) - 1)
        cshift_ref[sub, 0:span, :] = cext_ref[2 + sub:2 + sub + span, :]
        for a, jt in enumerate(taps):
            y = y + cshift_ref[sub, 8 * a:8 * a + tm, :] * convw_ref[jt:jt + 1, :]
    mean = jnp.mean(y, axis=-1, keepdims=True)
    yc = y - mean
    var = jnp.mean(yc * yc, axis=-1, keepdims=True)
    yn = yc * lax.rsqrt(var + LN_EPS) * lng_ref[...] + lnb_ref[...]
    o_d = jnp.dot(_silu(yn).astype(BF16), pw_ref[...], preferred_element_type=F32)

    acc = jnp.dot((oa_ref[0] * _silu(ga_ref[0])).astype(BF16), wout_ref[0:256, :], preferred_element_type=F32)
    acc = acc + jnp.dot(ob_ref[0].astype(BF16), wout_ref[256:512, :], preferred_element_type=F32)
    acc = acc + jnp.dot((o_c * _silu(gc_ref[0])).astype(BF16), wout_ref[512:768, :], preferred_element_type=F32)
    acc = acc + jnp.dot((o_d * _silu(gd_ref[0])).astype(BF16), wout_ref[768:1024, :], preferred_element_type=F32)
    xo_ref[0] = x_ref[0] + acc

    pool1_ref[0] = pext_ref[t_last + 1:t_last + pad_p, :]
    conv1_ref[0] = cext_ref[t_last + 2:t_last + pad_c, :]


def _mix(x, oa, ga, ob, uc, gc, ud, gd, pool0, conv0, w, batch, seq, tm, t_last, pos0):
    def tok(t, width):
        return t.reshape(batch, seq, width)

    def tok_spec(width):
        return pl.BlockSpec((1, tm, width), lambda b, i: (b, i, 0))

    def full(shape):
        return pl.BlockSpec(shape, lambda b, i: (0,) * len(shape))

    pool_spec = pl.BlockSpec((1, POOL_BUF, D_BRANCH), lambda b, i: (b, 0, 0))
    conv_spec = pl.BlockSpec((1, CONV_BUF, D_BRANCH), lambda b, i: (b, 0, 0))
    kern = functools.partial(_mix_kernel, tm=tm, t_last=t_last, pos0=pos0)
    return pl.pallas_call(
        kern,
        grid=(batch, seq // tm),
        in_specs=[tok_spec(D_MODEL), tok_spec(256), tok_spec(256), tok_spec(256), tok_spec(256), tok_spec(256),
                  tok_spec(512), tok_spec(256), pool_spec, conv_spec,
                  full((D_BRANCH, D_BRANCH)), full((1, D_BRANCH)), full((CONV_WIDTH, D_BRANCH)),
                  full((1, D_BRANCH)), full((1, D_BRANCH)), full((1, D_BRANCH)),
                  full((D_BRANCH, D_BRANCH)), full((D_MODEL, D_MODEL))],
        out_specs=[tok_spec(D_MODEL), pool_spec, conv_spec],
        out_shape=[jax.ShapeDtypeStruct((batch, seq, D_MODEL), F32),
                   jax.ShapeDtypeStruct((batch, POOL_BUF, D_BRANCH), F32),
                   jax.ShapeDtypeStruct((batch, CONV_BUF, D_BRANCH), F32)],
        scratch_shapes=[pltpu.VMEM((POOL_BUF + 1 + tm, D_BRANCH), F32),
                        pltpu.VMEM((CONV_BUF + 2 + tm, D_BRANCH), F32),
                        pltpu.VMEM((8, tm + 8, D_BRANCH), F32),
                        pltpu.VMEM((8, tm + 8 * ((CONV_WIDTH - 1) // 8), D_BRANCH), F32)],
        compiler_params=_params("parallel", "arbitrary"),
        name="mix",
    )(tok(x, D_MODEL), tok(oa, 256), tok(ga, 256), tok(ob, 256), tok(uc, 256), tok(gc, 256), tok(ud, 512),
      tok(gd, 256), pool0, conv0, w["pool_w"], w["pool_scale"], w["conv_w"], w["conv_b"], w["ln_g"], w["ln_b"],
      w["pw_out"], w["w_out"])


def _final_norm_kernel(x_ref, g_ref, o_ref):
    x = x_ref[...]
    ms = jnp.mean(x * x, axis=-1, keepdims=True)
    o_ref[...] = (x * lax.rsqrt(ms + RMS_EPS)) * g_ref[...]


def _final_norm(x, g, tm):
    n = x.shape[0]
    return pl.pallas_call(
        _final_norm_kernel,
        grid=(n // tm,),
        in_specs=[pl.BlockSpec((tm, D_MODEL), lambda i: (i, 0)), pl.BlockSpec((1, D_MODEL), lambda i: (0, 0))],
        out_specs=pl.BlockSpec((tm, D_MODEL), lambda i: (i, 0)),
        out_shape=jax.ShapeDtypeStruct((n, D_MODEL), F32),
        compiler_params=_params("parallel"),
        name="final_norm",
    )(x, g)


def _layer_weights(l, norm_g, w_in, fox_bf, rw_mu, rw_w0, rw_wup, rw_a0, rw_aup, rw_kk, rw_ka, rw_rk,
                   rw_gn_g, rw_gn_b, pool_w, pool_scale, conv_w, conv_b, ln_g, ln_b, pw_out, w_out):
    wi = w_in[l]
    o_f = 3 * D_BRANCH
    o_ga = o_f + N_HEADS
    main = jnp.concatenate([wi[:, 0:D_BRANCH], wi[:, o_ga:]], axis=1).astype(BF16)
    wkv_t = wi[:, D_BRANCH:o_f].T.astype(BF16)
    wf = jnp.zeros((8, D_MODEL), F32).at[0:N_HEADS].set(wi[:, o_f:o_ga].T).astype(BF16)
    bf = jnp.zeros((8, 1), F32).at[0:N_HEADS, 0].set(fox_bf[l])
    lora = jnp.zeros((2 * LORA, 2 * D_BRANCH), F32)
    lora = lora.at[0:LORA, 0:D_BRANCH].set(rw_wup[l]).at[LORA:, D_BRANCH:].set(rw_aup[l]).astype(BF16)
    pw_bd = jnp.zeros((D_BRANCH, D_BRANCH), F32)
    for g in range(len(POOL_WINDOWS)):
        gs = slice(g * HEAD_DIM, (g + 1) * HEAD_DIM)
        pw_bd = pw_bd.at[gs, gs].set(pool_w[l, g])

    def row(t):
        return t[l].reshape(1, -1)

    return dict(
        norm_g=row(norm_g), main=main, wkv_t=wkv_t, wf=wf, bf=bf,
        rwkv=dict(mu=row(rw_mu), w0=row(rw_w0), a0=row(rw_a0), lora=lora, kk=row(rw_kk), ka=row(rw_ka),
                  rk=row(rw_rk), gn_g=row(rw_gn_g), gn_b=row(rw_gn_b)),
        mix=dict(pool_w=pw_bd.astype(BF16), pool_scale=row(pool_scale), conv_w=conv_w[l], conv_b=row(conv_b),
                 ln_g=row(ln_g), ln_b=row(ln_b), pw_out=pw_out[l].astype(BF16), w_out=w_out[l].astype(BF16)))


def _stream_layer(x, lw, attend, wkv0, shift0, pool0, conv0, tiles, t_valid, pos0, flat_inproj):
    batch, seq, _ = x.shape
    x_in = x.reshape(1, batch * seq, D_MODEL) if flat_inproj else x
    *tok, kt, vt, lf_t = _inproj(x_in, lw["norm_g"], lw["main"], lw["wkv_t"], lw["wf"], lw["bf"], tiles["inproj"])
    q, ga, sh, gb, uc, gc, ud, gd = (t.reshape(batch, seq, t.shape[-1]) for t in tok)
    oa = attend(q, kt, vt, lf_t)
    if flat_inproj:
        prev0 = jnp.pad(shift0[:, None, :], ((0, 0), (0, seq - 1), (0, 0))).reshape(1, batch * seq, SHIFT_W)
        ob, wkv1 = _rwkv(sh.reshape(1, batch * seq, SHIFT_W), gb.reshape(1, batch * seq, D_BRANCH), wkv0, prev0,
                         lw["rwkv"], tiles["rwkv"], seq, t_valid, False)
        ob = ob.reshape(batch, seq, D_BRANCH)
    else:
        ob, wkv1 = _rwkv(sh, gb, wkv0, shift0[:, None, :], lw["rwkv"], tiles["rwkv"], tiles["chunk"],
                         tiles["chunk"], True)
    t_last = t_valid - (seq - tiles["mix"])
    x1, pool1, conv1 = _mix(x, oa, ga, ob, uc, gc, ud, gd, pool0, conv0, lw["mix"], batch, seq, tiles["mix"],
                            t_last, pos0)
    return x1, kt, vt, sh, lf_t, wkv1, pool1, conv1


def kernel(x_prompt, x_sample, cache_k, cache_v, cache_logf, state_wkv, state_shift, state_pool, state_conv,
           page_table, norm_g, w_in, fox_bf, rw_mu, rw_w0, rw_wup, rw_a0, rw_aup, rw_kk, rw_ka, rw_rk,
           rw_gn_g, rw_gn_b, pool_w, pool_scale, conv_w, conv_b, ln_g, ln_b, pw_out, w_out, final_norm_g):
    bp, tp, _ = x_prompt.shape
    bs, ts, _ = x_sample.shape
    depth = w_in.shape[0]
    n_phys = cache_k.shape[0]
    past = page_table.shape[1] * PAGE
    weights = (norm_g, w_in, fox_bf, rw_mu, rw_w0, rw_wup, rw_a0, rw_aup, rw_kk, rw_ka, rw_rk, rw_gn_g, rw_gn_b,
               pool_w, pool_scale, conv_w, conv_b, ln_g, ln_b, pw_out, w_out)

    tile_p = min(256, tp)
    tiles_p = dict(inproj=tile_p, rwkv=tile_p, chunk=min(64, tile_p), mix=tile_p)
    tiles_s = dict(inproj=min(256, bs * SAMPLE_PAD), rwkv=min(64, bs * SAMPLE_PAD), chunk=SAMPLE_PAD, mix=SAMPLE_PAD)

    xp = x_prompt
    xs = jnp.pad(x_sample, ((0, 0), (0, SAMPLE_PAD - ts), (0, 0)))
    cache_kt = jnp.transpose(cache_k, (0, 1, 3, 4, 2))
    cache_vt = jnp.transpose(cache_v, (0, 1, 3, 4, 2))
    cache_lft = jnp.transpose(cache_logf, (0, 1, 3, 2))

    wkv_zero = jnp.zeros((bp, N_HEADS, HEAD_DIM, HEAD_DIM), F32)
    shift_zero = jnp.zeros((bp, SHIFT_W), F32)
    pool_zero = jnp.zeros((bp, POOL_BUF, D_BRANCH), F32)
    conv_zero = jnp.zeros((bp, CONV_BUF, D_BRANCH), F32)

    def sample_heads(t):
        return t.reshape(N_HEADS, HEAD_DIM, bs, SAMPLE_PAD).transpose(2, 0, 1, 3)

    def pad_keys(t):
        return jnp.pad(t, [(0, 0)] * (t.ndim - 1) + [(0, PAGE - SAMPLE_PAD)])

    outs_p, outs_s = [], []
    for l in range(depth):
        lw = _layer_weights(l, *weights)

        attend_p = lambda q, kt, vt, lf_t: _fox_prompt(q, kt, vt, lf_t, tile_p)
        xp, kt_p, vt_p, sh_p, lf_p, wkv_p, pool_p, conv_p = _stream_layer(
            xp, lw, attend_p, wkv_zero, shift_zero, pool_zero, conv_zero, tiles_p, tp, 0, False)
        outs_p.append((kt_p.reshape(bp, N_HEADS, HEAD_DIM, tp), vt_p.reshape(bp, N_HEADS, HEAD_DIM, tp),
                       lf_p[:, 0:N_HEADS], wkv_p, sh_p[:, tp - 1], pool_p, conv_p))

        def attend_s(q, kt, vt, lf_t, l=l):
            lf_new = lf_t[0, 0:N_HEADS].reshape(N_HEADS, bs, SAMPLE_PAD).transpose(1, 0, 2)
            return _fox_sample(q, pad_keys(sample_heads(kt)), pad_keys(sample_heads(vt)), pad_keys(lf_new),
                               cache_kt, cache_vt, cache_lft, page_table, l, ts)

        xs, kt_s, vt_s, sh_s, lf_s, wkv_s, pool_s, conv_s = _stream_layer(
            xs, lw, attend_s, state_wkv[:, l], state_shift[:, l], state_pool[:, l], state_conv[:, l],
            tiles_s, ts, past, True)
        outs_s.append((sample_heads(kt_s).transpose(0, 3, 1, 2)[:, 0:ts],
                       sample_heads(vt_s).transpose(0, 3, 1, 2)[:, 0:ts],
                       lf_s[0, 0:N_HEADS].reshape(N_HEADS, bs, SAMPLE_PAD).transpose(1, 2, 0)[:, 0:ts],
                       wkv_s, sh_s[:, ts - 1], pool_s, conv_s))

    fg = final_norm_g.reshape(1, D_MODEL)
    y_prompt = _final_norm(xp.reshape(bp * tp, D_MODEL), fg, tile_p).reshape(bp, tp, D_MODEL)
    y_sample = _final_norm(xs.reshape(bs * SAMPLE_PAD, D_MODEL), fg, min(256, bs * SAMPLE_PAD))
    y_sample = y_sample.reshape(bs, SAMPLE_PAD, D_MODEL)[:, 0:ts]

    def stk(outs, i):
        return jnp.stack([o[i] for o in outs], axis=1)

    k_prompt = stk(outs_p, 0).transpose(0, 1, 4, 2, 3)
    v_prompt = stk(outs_p, 1).transpose(0, 1, 4, 2, 3)
    logf_prompt = stk(outs_p, 2).transpose(0, 1, 3, 2)
    return (y_prompt, y_sample, k_prompt, v_prompt, logf_prompt,
            *(stk(outs_p, i) for i in range(3, 7)),
            *(stk(outs_s, i) for i in range(7)))
```

```python
import functools

import jax
import jax.numpy as jnp
from jax import lax
from jax.experimental import pallas as pl
from jax.experimental.pallas import tpu as pltpu

F32 = jnp.float32
BF16 = jnp.bfloat16

D_MODEL = 1024
D_BRANCH = 256
HEAD_DIM = 64
N_HEADS = 4
LORA = 64
SHIFT_W = 3 * D_BRANCH + 2 * LORA
PAGE = 128
POOL_WINDOWS = (2, 4, 8, 16)
POOL_BUF = 15
CONV_WIDTH = 31
CONV_BUF = 30
RMS_EPS = 1e-6
LN_EPS = 1e-5
GN_EPS = 64e-5
ATTN_SCALE = HEAD_DIM ** -0.5
NEG_BIG = -1e30
SAMPLE_PAD = 8
PAGES_PER_STEP = 8
RWKV_PIECE = 64
RWKV_GROUP = 32
VMEM_LIMIT = 56 * 1024 * 1024

NN = ((1,), (0,))
NT = ((1,), (1,))
TN = ((0,), (0,))


def _dg(a, b, dims=NN):
    return lax.dot_general(a, b, (dims, ((), ())), preferred_element_type=F32)


def _split3(x):
    hi = x.astype(BF16)
    r1 = x - hi.astype(F32)
    mid = r1.astype(BF16)
    lo = (r1 - mid.astype(F32)).astype(BF16)
    return hi, mid, lo


def _dot_f32(a, b, dims=NN):
    ah, am, _ = _split3(a)
    bh, bm, _ = _split3(b)
    return _dg(ah, bh, dims) + (_dg(ah, bm, dims) + _dg(am, bh, dims))


def _dot_exact_rhs(a, b_bf16, dims=NN):
    ah, am, al = _split3(a)
    return _dg(ah, b_bf16, dims) + (_dg(am, b_bf16, dims) + _dg(al, b_bf16, dims))


def _dot_exact_lhs(a_bf16, b, dims=NN):
    bh, bm, bl = _split3(b)
    return _dg(a_bf16, bh, dims) + (_dg(a_bf16, bm, dims) + _dg(a_bf16, bl, dims))


def _group_sum(a, ones_bf16):
    hi = a.astype(BF16)
    mid = (a - hi.astype(F32)).astype(BF16)
    return _dg(hi, ones_bf16) + _dg(mid, ones_bf16)


def _sigmoid(x):
    return 1.0 / (1.0 + jnp.exp(-x))


def _silu(x):
    return x * _sigmoid(x)


def _softplus(x):
    return jnp.maximum(x, 0.0) + jnp.log1p(jnp.exp(-jnp.abs(x)))


def _iota(shape, dim):
    return lax.broadcasted_iota(jnp.int32, shape, dim)


def _params(*sem):
    return pltpu.CompilerParams(dimension_semantics=sem, vmem_limit_bytes=VMEM_LIMIT)


_IN_SEGS = (("q", 256), ("ga", 256), ("sh", SHIFT_W), ("gb", 256),
            ("uc", 256), ("gc", 256), ("ud", 512), ("gd", 256))
_IN_COLS = sum(s for _, s in _IN_SEGS)


def _inproj_kernel(x_ref, g_ref, w_ref, wkv_ref, wf_ref, bf_ref, *out_refs):
    x = x_ref[0]
    ms = jnp.mean(x * x, axis=-1, keepdims=True)
    h = ((x * lax.rsqrt(ms + RMS_EPS)) * g_ref[...]).astype(BF16)
    off = 0
    for (_, size), o_ref in zip(_IN_SEGS, out_refs[:-3]):
        o_ref[0] = jnp.dot(h, w_ref[:, off:off + size], preferred_element_type=F32)
        off += size
    kt_ref, vt_ref, lf_ref = out_refs[-3:]
    kt_ref[0] = _dg(wkv_ref[0:D_BRANCH, :], h, NT)
    vt_ref[0] = _dg(wkv_ref[D_BRANCH:2 * D_BRANCH, :], h, NT)
    f = _dg(wf_ref[...], h, NT) + bf_ref[...]
    lf_ref[0] = -_softplus(-f)


def _inproj(x, g, w, wkv_t, wf, bf, tm):
    batch, seq, _ = x.shape
    out_shape = [jax.ShapeDtypeStruct((batch, seq, s), F32) for _, s in _IN_SEGS]
    out_shape += [jax.ShapeDtypeStruct((batch, D_BRANCH, seq), F32)] * 2
    out_shape.append(jax.ShapeDtypeStruct((batch, 8, seq), F32))
    out_specs = [pl.BlockSpec((1, tm, s), lambda b, i: (b, i, 0)) for _, s in _IN_SEGS]
    out_specs += [pl.BlockSpec((1, D_BRANCH, tm), lambda b, i: (b, 0, i))] * 2
    out_specs.append(pl.BlockSpec((1, 8, tm), lambda b, i: (b, 0, i)))

    def full(shape):
        return pl.BlockSpec(shape, lambda b, i: (0,) * len(shape))

    return pl.pallas_call(
        _inproj_kernel,
        grid=(batch, seq // tm),
        in_specs=[pl.BlockSpec((1, tm, D_MODEL), lambda b, i: (b, i, 0)),
                  full((1, D_MODEL)), full((D_MODEL, _IN_COLS)), full((2 * D_BRANCH, D_MODEL)),
                  full((8, D_MODEL)), full((8, 1))],
        out_specs=out_specs,
        out_shape=out_shape,
        compiler_params=_params("parallel", "parallel"),
        name="inproj",
    )(x, g, w, wkv_t, wf, bf)


def _fox_prompt_kernel(q_ref, kt_ref, vt_ref, lf_ref, o_ref, c_ref, kb_ref, vb_ref, m_ref, acc_ref, *, tq, seq):
    i = pl.program_id(1)
    heads = [slice(h * HEAD_DIM, (h + 1) * HEAD_DIM) for h in range(N_HEADS)]

    @pl.when(i == 0)
    def _per_sequence():
        upper = (_iota((128, 128), 0) <= _iota((128, 128), 1)).astype(BF16)
        carry = jnp.zeros((8, 1), F32)
        for blk in range(seq // 128):
            cs = _dot_exact_rhs(lf_ref[0, :, blk * 128:(blk + 1) * 128], upper) + carry
            c_ref[:, blk * 128:(blk + 1) * 128] = cs
            carry = cs[:, 127:128]
        kb_ref[...] = kt_ref[0].astype(BF16)
        for h, hs in enumerate(heads):
            vb_ref[h, 0:HEAD_DIM, :] = vt_ref[0, hs, :].astype(BF16)
            vb_ref[h, HEAD_DIM:2 * HEAD_DIM, :] = jnp.ones((HEAD_DIM, seq), BF16)

    row0 = pl.multiple_of(i * tq, tq)
    c_tile = c_ref[:, pl.ds(row0, tq)]
    causal = _iota((tq, tq), 1) <= _iota((tq, tq), 0)

    q_all = q_ref[0] * ATTN_SCALE
    q_heads = [q_all[:, hs].astype(BF16) for hs in heads]
    m_ref[...] = jnp.full(m_ref.shape, NEG_BIG, F32)
    acc_ref[...] = jnp.zeros(acc_ref.shape, F32)

    def tile_update(col0, masked):
        for h, hs in enumerate(heads):
            kh = kb_ref[hs, pl.ds(col0, tq)]
            vh = vb_ref[h, :, pl.ds(col0, tq)]
            s = _dg(q_heads[h], kh) + (c_tile[h:h + 1, 0:1] - c_ref[h:h + 1, pl.ds(col0, tq)])
            if masked:
                s = jnp.where(causal, s, NEG_BIG)
            m_old = m_ref[h]
            m_new = jnp.maximum(m_old, jnp.max(s, axis=-1, keepdims=True))
            p = jnp.exp(s - jnp.tile(m_new, (1, tq // 128)))
            alpha = jnp.exp(m_old - m_new)
            acc_ref[h] = alpha * acc_ref[h] + _dg(p.astype(BF16), vh, NT)
            m_ref[h] = m_new

    def body(j, carry):
        tile_update(pl.multiple_of(j * tq, tq), False)
        return carry

    lax.fori_loop(0, i, body, 0)
    tile_update(row0, True)
    for h, hs in enumerate(heads):
        a = acc_ref[h]
        o_ref[0, :, hs] = a[:, 0:HEAD_DIM] / a[:, HEAD_DIM:2 * HEAD_DIM]


def _fox_prompt(q, kt, vt, lf_t, tq):
    batch, seq, _ = q.shape
    kern = functools.partial(_fox_prompt_kernel, tq=tq, seq=seq)
    return pl.pallas_call(
        kern,
        grid=(batch, seq // tq),
        in_specs=[pl.BlockSpec((1, tq, D_BRANCH), lambda b, i: (b, i, 0)),
                  pl.BlockSpec((1, D_BRANCH, seq), lambda b, i: (b, 0, 0)),
                  pl.BlockSpec((1, D_BRANCH, seq), lambda b, i: (b, 0, 0)),
                  pl.BlockSpec((1, 8, seq), lambda b, i: (b, 0, 0))],
        out_specs=pl.BlockSpec((1, tq, D_BRANCH), lambda b, i: (b, i, 0)),
        out_shape=jax.ShapeDtypeStruct((batch, seq, D_BRANCH), F32),
        scratch_shapes=[pltpu.VMEM((8, seq), F32),
                        pltpu.VMEM((D_BRANCH, seq), BF16), pltpu.VMEM((N_HEADS, 2 * HEAD_DIM, seq), BF16),
                        pltpu.VMEM((N_HEADS, tq, 128), F32), pltpu.VMEM((N_HEADS, tq, 2 * HEAD_DIM), F32)],
        compiler_params=_params("parallel", "arbitrary"),
        name="fox_prompt",
    )(q, kt, vt, lf_t)


def _fox_sample_kernel(pt_ref, q_ref, kn_ref, vn_ref, lfn_ref, ck_ref, cv_ref, clf_ref, o_ref,
                       kbuf, vbuf, lfbuf, suffix_ref, total_ref, s_ref, ksem, vsem, lfsem,
                       *, layer, n_pages, t_valid):
    b = pl.program_id(0)
    slot = b % 2
    rows = N_HEADS * SAMPLE_PAD
    group = PAGES_PER_STEP

    def page_copies(seq, slot_, p):
        page = pt_ref[seq, p]
        return (pltpu.make_async_copy(ck_ref.at[page, layer], kbuf.at[slot_, p], ksem.at[slot_]),
                pltpu.make_async_copy(cv_ref.at[page, layer], vbuf.at[slot_, p], vsem.at[slot_]),
                pltpu.make_async_copy(clf_ref.at[page, layer], lfbuf.at[slot_, p, 0:N_HEADS], lfsem.at[slot_]))

    def start_fetch(seq, slot_):
        def body(p, carry):
            for cp in page_copies(seq, slot_, p):
                cp.start()
            return carry
        lax.fori_loop(0, n_pages, body, 0)

    def wait_fetch(seq, slot_):
        def body(p, carry):
            for cp in page_copies(seq, slot_, p):
                cp.wait()
            return carry
        lax.fori_loop(0, n_pages, body, 0)

    @pl.when(b == 0)
    def _first():
        lfbuf[...] = jnp.zeros(lfbuf.shape, F32)
        start_fetch(0, 0)

    @pl.when(b + 1 < pl.num_programs(0))
    def _prefetch_next():
        start_fetch(b + 1, 1 - slot)

    wait_fetch(b, slot)

    q_all = q_ref[0] * ATTN_SCALE
    q_heads = [q_all[:, h * HEAD_DIM:(h + 1) * HEAD_DIM].astype(BF16) for h in range(N_HEADS)]

    key4 = _iota((N_HEADS, PAGE), 1)
    lfbuf[slot, n_pages, 0:N_HEADS, :] = jnp.where(key4 < t_valid, lfn_ref[0], 0.0)
    lf_all = lfbuf[slot].reshape((n_pages + 1) * 8, PAGE)
    later = (_iota((PAGE, PAGE), 0) > _iota((PAGE, PAGE), 1)).astype(BF16)
    suffix_ref[...] = _dot_exact_rhs(lf_all, later)
    total_ref[...] = _dot_exact_rhs(lf_all, jnp.ones((PAGE, PAGE), BF16))

    def scores(kt, bias8, col0, mask):
        for h in range(N_HEADS):
            s = _dg(q_heads[h], kt[h].astype(BF16)) + bias8[h:h + 1, :]
            if mask is not None:
                s = jnp.where(mask, s, NEG_BIG)
            s_ref[h * SAMPLE_PAD:(h + 1) * SAMPLE_PAD, pl.ds(col0, PAGE)] = s

    new0 = n_pages * 8
    key = _iota((SAMPLE_PAD, PAGE), 1)
    qry = _iota((SAMPLE_PAD, PAGE), 0)
    scores(kn_ref[0], suffix_ref[new0:new0 + 8, :], n_pages * PAGE,
           jnp.logical_and(key <= qry, key < t_valid))

    def score_group(g, running):
        for i in range(group):
            p = n_pages - 1 - (g * group + i)
            r0 = pl.multiple_of(p * 8, 8)
            scores(kbuf[slot, p], suffix_ref[pl.ds(r0, 8), :] + running, pl.multiple_of(p * PAGE, PAGE), None)
            running = running + total_ref[pl.ds(r0, 8), :]
        return running

    lax.fori_loop(0, n_pages // group, score_group, total_ref[new0:new0 + 8, :])

    s = s_ref[...]
    m = jnp.max(s, axis=-1, keepdims=True)
    p_all = jnp.exp(s - m)
    denom = jnp.sum(p_all, axis=-1, keepdims=True)
    s_ref[...] = p_all

    def weighted(vt, col0):
        return jnp.concatenate(
            [_dg(s_ref[h * SAMPLE_PAD:(h + 1) * SAMPLE_PAD, pl.ds(col0, PAGE)].astype(BF16),
                 vt[h].astype(BF16), NT) for h in range(N_HEADS)], axis=0)

    def value_group(g, acc):
        for i in range(group):
            p = g * group + i
            acc = acc + weighted(vbuf[slot, p], pl.multiple_of(p * PAGE, PAGE))
        return acc

    acc = lax.fori_loop(0, n_pages // group, value_group, weighted(vn_ref[0], n_pages * PAGE))
    out = acc / denom
    for h in range(N_HEADS):
        o_ref[0, :, h * HEAD_DIM:(h + 1) * HEAD_DIM] = out[h * SAMPLE_PAD:(h + 1) * SAMPLE_PAD]


def _fox_sample(q, kt_new, vt_new, lf_new, cache_kt, cache_vt, cache_lft, page_table, layer, t_valid):
    batch, n_pages = page_table.shape
    rows = N_HEADS * SAMPLE_PAD

    def per_seq(shape):
        return pl.BlockSpec((1,) + shape, lambda b, pt: (b,) + (0,) * len(shape))

    in_specs = [per_seq((SAMPLE_PAD, D_BRANCH)), per_seq((N_HEADS, HEAD_DIM, PAGE)),
                per_seq((N_HEADS, HEAD_DIM, PAGE)), per_seq((N_HEADS, PAGE)),
                pl.BlockSpec(memory_space=pl.ANY), pl.BlockSpec(memory_space=pl.ANY),
                pl.BlockSpec(memory_space=pl.ANY)]
    kern = functools.partial(_fox_sample_kernel, layer=layer, n_pages=n_pages, t_valid=t_valid)
    return pl.pallas_call(
        kern,
        grid_spec=pltpu.PrefetchScalarGridSpec(
            num_scalar_prefetch=1,
            grid=(batch,),
            in_specs=in_specs,
            out_specs=per_seq((SAMPLE_PAD, D_BRANCH)),
            scratch_shapes=[pltpu.VMEM((2, n_pages, N_HEADS, HEAD_DIM, PAGE), F32),
                            pltpu.VMEM((2, n_pages, N_HEADS, HEAD_DIM, PAGE), F32),
                            pltpu.VMEM((2, n_pages + 1, 8, PAGE), F32),
                            pltpu.VMEM(((n_pages + 1) * 8, PAGE), F32),
                            pltpu.VMEM(((n_pages + 1) * 8, PAGE), F32),
                            pltpu.VMEM((rows, (n_pages + 1) * PAGE), F32),
                            pltpu.SemaphoreType.DMA((2,)), pltpu.SemaphoreType.DMA((2,)),
                            pltpu.SemaphoreType.DMA((2,))]),
        out_shape=jax.ShapeDtypeStruct((batch, SAMPLE_PAD, D_BRANCH), F32),
        compiler_params=_params("arbitrary"),
        name="fox_sample",
    )(page_table, q, kt_new, vt_new, lf_new, cache_kt, cache_vt, cache_lft)


def _unit_lower_inverses(ns, size, chunk):
    r = _iota((size, size), 0)
    c = _iota((size, size), 1)
    eye = (r == c).astype(F32)
    pair = (r // 2) == (c // 2)
    ms = [eye - jnp.where(pair, n.astype(F32), 0.0) for n in ns]
    blk = 2
    while blk < chunk:
        off = jnp.logical_and((r // (2 * blk)) == (c // (2 * blk)), (r // blk) != (c // blk))
        mbs = [m.astype(BF16) for m in ms]
        ts = [_dg(mb, n).astype(BF16) for mb, n in zip(mbs, ns)]
        ts = [_dg(t, mb) for t, mb in zip(ts, mbs)]
        ms = [m - jnp.where(off, t, 0.0) for m, t in zip(ms, ts)]
        blk *= 2
    return ms


def _rwkv_kernel(sh_ref, gb_ref, s0_ref, prev0_ref, mu_ref, w0_ref, a0_ref, lora_ref, kkw_ref, ka_ref,
                 rk_ref, gng_ref, gnb_ref, o_ref, s1_ref, ext_ref, st_ref, oacc_ref, *, tb, chunk, t_valid,
                 carried):
    n_blk = tb // chunk
    sh = sh_ref[0]
    if carried:
        i = pl.program_id(1)

        @pl.when(i == 0)
        def _init():
            st_ref[...] = s0_ref[0]
            ext_ref[0:8, :] = jnp.zeros((8, SHIFT_W), F32)
            ext_ref[7:8, :] = prev0_ref[0]

        @pl.when(i > 0)
        def _carry_shift():
            ext_ref[7:8, :] = ext_ref[tb + 7:tb + 8, :]

        ext_ref[8:8 + tb, :] = sh
        prev = ext_ref[7:7 + tb, :]
    else:
        ext_ref[0:8, :] = jnp.zeros((8, SHIFT_W), F32)
        ext_ref[8:8 + tb, :] = sh
        first = (_iota((tb, 1), 0) % chunk) == 0
        prev = jnp.where(first, prev0_ref[0], ext_ref[7:7 + tb, :])
    xm = sh + mu_ref[...] * (prev - sh)
    r = xm[:, 0:256]
    k = xm[:, 256:512]
    v = xm[:, 512:768]
    wa = xm[:, 768:896]
    wa = jnp.where(_iota((tb, 2 * LORA), 1) < LORA, jnp.tanh(wa), wa)
    lora = jnp.dot(wa.astype(BF16), lora_ref[...], preferred_element_type=F32)
    w_log = -_softplus(-(w0_ref[...] + lora[:, 0:256])) - 0.5
    log_decay = -jnp.exp(w_log)
    a = _sigmoid(a0_ref[...] + lora[:, 256:512])

    grp = ((_iota((D_BRANCH, D_BRANCH), 0) // HEAD_DIM) == (_iota((D_BRANCH, D_BRANCH), 1) // HEAD_DIM)).astype(BF16)
    kk = k * kkw_ref[...]
    kk = kk * lax.rsqrt(jnp.maximum(_group_sum(kk * kk, grp), 1e-12))
    kp = k * (1.0 + (a - 1.0) * ka_ref[...])
    beta = kk * a
    bonus = _group_sum(r * kp * rk_ref[...], grp) * v
    if t_valid < chunk:
        keep = ((_iota((tb, 1), 0) % chunk) < t_valid).astype(F32)
        log_decay = log_decay * keep
        kp = kp * keep
        beta = beta * keep

    rr = _iota((tb, tb), 0)
    cc = _iota((tb, tb), 1)
    incl = jnp.logical_and((rr // chunk) == (cc // chunk), rr >= cc)

    lc = _dot_exact_lhs(incl.astype(BF16), log_decay)
    lc_tot = jnp.concatenate(
        [jnp.broadcast_to(lc[(c + 1) * chunk - 1:(c + 1) * chunk, :], (chunk, D_BRANCH)) for c in range(n_blk)],
        axis=0)
    to_end = jnp.exp(lc_tot - lc)
    g_inv = jnp.exp(-lc)
    g_end = jnp.exp(lc_tot)
    qc = kk * jnp.exp(lc - log_decay)
    rc = r * jnp.exp(lc)
    kc = kp * g_inv
    bc = beta * g_inv
    kd = kp * to_end
    bd = beta * to_end
    eye_k = (_iota((HEAD_DIM, HEAD_DIM), 0) == _iota((HEAD_DIM, HEAD_DIM), 1)).astype(F32)

    def b16(t):
        return t.astype(BF16)

    mat = min(tb, RWKV_PIECE)
    n_sub = tb // mat
    rg = _iota((2 * mat, 2 * mat), 0)
    rl = rg % mat
    cl = _iota((2 * mat, 2 * mat), 1) % mat
    g_mask = jnp.logical_and((rl // chunk) == (cl // chunk), rl + rg // mat > cl)
    pieces = [(sub, h) for sub in range(n_sub) for h in range(N_HEADS)]

    def piece(t, sub, h):
        return t[sub * mat:(sub + 1) * mat, h * HEAD_DIM:(h + 1) * HEAD_DIM]

    zero_b = jnp.zeros((mat, HEAD_DIM), BF16)
    r_f, v_f, v_b, kd_f, bd_f, w_f, y_f, o0, rp = ([] for _ in range(9))
    for g0 in range(0, len(pieces), RWKV_GROUP):
        grp_pieces = pieces[g0:g0 + RWKV_GROUP]
        r_g = [piece(rc, *ph) for ph in grp_pieces]
        v_g = [piece(v, *ph) for ph in grp_pieces]
        q_b = [b16(piece(qc, *ph)) for ph in grp_pieces]
        k_b = [b16(piece(kc, *ph)) for ph in grp_pieces]
        b_b = [b16(piece(bc, *ph)) for ph in grp_pieces]
        r_b = [b16(t) for t in r_g]
        vb_g = [b16(t) for t in v_g]
        g_all = [b16(jnp.where(g_mask, _dg(jnp.concatenate([q, rr_], axis=0), jnp.concatenate([bb, k], axis=0), NT),
                               0.0)) for q, rr_, bb, k in zip(q_b, r_b, b_b, k_b)]
        g_top = [g[0:mat] for g in g_all]
        g_bot = [g[mat:2 * mat] for g in g_all]
        inv = [b16(t) for t in _unit_lower_inverses([g[:, 0:mat] for g in g_top], mat, chunk)]
        w_g = [_dg(m_, q) for m_, q in zip(inv, q_b)]
        av_b = [b16(_dg(g, jnp.concatenate([zero_b, vv], axis=0))) for g, vv in zip(g_top, vb_g)]
        y_g = [_dg(m_, av) for m_, av in zip(inv, av_b)]
        o0 += [_dg(g, jnp.concatenate([b16(-yy), vv], axis=0)) for g, yy, vv in zip(g_bot, y_g, vb_g)]
        rp += [rr_ - _dg(g, jnp.concatenate([b16(ww), zero_b], axis=0)) for rr_, g, ww in zip(r_g, g_bot, w_g)]
        r_f += r_g
        v_f += v_g
        v_b += vb_g
        w_f += w_g
        y_f += y_g
        kd_f += [piece(kd, *ph) for ph in grp_pieces]
        bd_f += [piece(bd, *ph) for ph in grp_pieces]

    blocks_per_sub = mat // chunk
    for h in range(N_HEADS):
        hs = slice(h * HEAD_DIM, (h + 1) * HEAD_DIM)
        state = st_ref[h] if carried else None
        for c in range(n_blk):
            sub, loc = divmod(c, blocks_per_sub)
            idx = sub * N_HEADS + h
            ls = slice(loc * chunk, (loc + 1) * chunk)
            sl = slice(c * chunk, (c + 1) * chunk)
            bd_c = b16(bd_f[idx][ls])
            p_mat = eye_k * g_end[c * chunk:c * chunk + 1, hs] - _dg(b16(w_f[idx][ls]), bd_c, TN)
            z_mat = _dg(v_b[idx][ls] if chunk % 16 == 0 else b16(v_f[idx][ls]), b16(kd_f[idx][ls]), TN) \
                - _dg(b16(y_f[idx][ls]), bd_c, TN)
            s_in = state if carried else s0_ref[c, h]
            oacc_ref[sl, hs] = o0[idx][ls] + _dg(b16(rp[idx][ls]), b16(s_in), NT)
            s_out = _dot_f32(s_in, p_mat) + z_mat
            if carried:
                state = s_out
            else:
                s1_ref[c, h] = s_out
        if carried:
            st_ref[h] = state

    o = oacc_ref[...]
    mean = _group_sum(o, grp) * (1.0 / HEAD_DIM)
    d = o - mean
    var = _group_sum(d * d, grp) * (1.0 / HEAD_DIM)
    o = d * lax.rsqrt(var + GN_EPS) * gng_ref[...] + gnb_ref[...] + bonus
    o_ref[0] = o * _silu(gb_ref[0])
    if carried:
        s1_ref[0] = st_ref[...]


def _rwkv(sh, gb, s0, prev0, w, tb, chunk, t_valid, carried):
    batch, seq, _ = sh.shape
    kern = functools.partial(_rwkv_kernel, tb=tb, chunk=chunk, t_valid=t_valid, carried=carried)

    def row(width):
        return pl.BlockSpec((1, width), lambda b, i: (0, 0))

    if carried:
        state_spec = pl.BlockSpec((1, N_HEADS, HEAD_DIM, HEAD_DIM), lambda b, i: (b, 0, 0, 0))
        prev_spec = pl.BlockSpec((1, 1, SHIFT_W), lambda b, i: (b, 0, 0))
    else:
        state_spec = pl.BlockSpec((tb // chunk, N_HEADS, HEAD_DIM, HEAD_DIM), lambda b, i: (i, 0, 0, 0))
        prev_spec = pl.BlockSpec((1, tb, SHIFT_W), lambda b, i: (b, i, 0))
    return pl.pallas_call(
        kern,
        grid=(batch, seq // tb),
        in_specs=[pl.BlockSpec((1, tb, SHIFT_W), lambda b, i: (b, i, 0)),
                  pl.BlockSpec((1, tb, D_BRANCH), lambda b, i: (b, i, 0)),
                  state_spec, prev_spec,
                  row(SHIFT_W), row(D_BRANCH), row(D_BRANCH),
                  pl.BlockSpec((2 * LORA, 2 * D_BRANCH), lambda b, i: (0, 0)),
                  row(D_BRANCH), row(D_BRANCH), row(D_BRANCH), row(D_BRANCH), row(D_BRANCH)],
        out_specs=[pl.BlockSpec((1, tb, D_BRANCH), lambda b, i: (b, i, 0)), state_spec],
        out_shape=[jax.ShapeDtypeStruct((batch, seq, D_BRANCH), F32),
                   jax.ShapeDtypeStruct(s0.shape, F32)],
        scratch_shapes=[pltpu.VMEM((tb + 8, SHIFT_W), F32),
                        pltpu.VMEM((N_HEADS, HEAD_DIM, HEAD_DIM), F32),
                        pltpu.VMEM((tb, D_BRANCH), F32)],
        compiler_params=_params("parallel", "arbitrary"),
        name="rwkv",
    )(sh, gb, s0, prev0, w["mu"], w["w0"], w["a0"], w["lora"], w["kk"], w["ka"],
      w["rk"], w["gn_g"], w["gn_b"])


def _mix_kernel(x_ref, oa_ref, ga_ref, ob_ref, uc_ref, gc_ref, ud_ref, gd_ref, pool0_ref, conv0_ref,
                poolw_ref, pscale_ref, convw_ref, convb_ref, lng_ref, lnb_ref, pw_ref, wout_ref, fg_ref,
                xo_ref, pool1_ref, conv1_ref, pext_ref, cext_ref, pshift_ref, cshift_ref,
                *, tm, t_last, pos0, final):
    i = pl.program_id(1)
    pad_p = POOL_BUF + 1
    pad_c = CONV_BUF + 2

    @pl.when(i == 0)
    def _init():
        pext_ref[0:pad_p, :] = jnp.zeros((pad_p, D_BRANCH), F32)
        cext_ref[0:pad_c, :] = jnp.zeros((pad_c, D_BRANCH), F32)
        pext_ref[1:pad_p, :] = pool0_ref[0]
        cext_ref[2:pad_c, :] = conv0_ref[0]

    @pl.when(i > 0)
    def _carry():
        pext_ref[1:pad_p, :] = pext_ref[tm + 1:tm + pad_p, :]
        cext_ref[2:pad_c, :] = cext_ref[tm + 2:tm + pad_c, :]

    uc = uc_ref[0]
    pext_ref[pad_p:pad_p + tm, :] = uc
    ud = ud_ref[0]
    cext_ref[pad_c:pad_c + tm, :] = ud[:, 0:D_BRANCH] * _sigmoid(ud[:, D_BRANCH:2 * D_BRANCH])

    done_p = set()

    def back(d):
        a, sub = divmod(pad_p - d, 8)
        if sub not in done_p:
            done_p.add(sub)
            pshift_ref[sub] = pext_ref[sub:sub + tm + 8, :]
        return pshift_ref[sub, 8 * a:8 * a + tm, :]

    win2 = uc + back(1)
    win4 = win2 + back(2) + back(3)
    win8 = win4
    for d in range(4, 8):
        win8 = win8 + back(d)
    win16 = win8
    for d in range(8, 16):
        win16 = win16 + back(d)
    group = _iota((tm, D_BRANCH), 1) // HEAD_DIM
    win = jnp.where(group == 0, win2, jnp.where(group == 1, win4, jnp.where(group == 2, win8, win16)))
    width = jnp.where(group == 0, 2, jnp.where(group == 1, 4, jnp.where(group == 2, 8, 16)))
    pos = pos0 + i * tm + _iota((tm, D_BRANCH), 0)
    cnt = jnp.minimum(pos + 1, width).astype(F32)
    pooled = win / cnt - uc
    o_c = jnp.dot(pooled.astype(BF16), poolw_ref[...], preferred_element_type=F32) * pscale_ref[...]

    y = jnp.zeros((tm, D_BRANCH), F32) + convb_ref[...]
    for sub in range(8):
        taps = range(sub, CONV_WIDTH, 8)
        span = tm + 8 * (len(taps) - 1)
        cshift_ref[sub, 0:span, :] = cext_ref[2 + sub:2 + sub + span, :]
        for a, jt in enumerate(taps):
            y = y + cshift_ref[sub, 8 * a:8 * a + tm, :] * convw_ref[jt:jt + 1, :]
    mean = jnp.mean(y, axis=-1, keepdims=True)
    yc = y - mean
    var = jnp.mean(yc * yc, axis=-1, keepdims=True)
    yn = yc * lax.rsqrt(var + LN_EPS) * lng_ref[...] + lnb_ref[...]
    o_d = jnp.dot(_silu(yn).astype(BF16), pw_ref[...], preferred_element_type=F32)

    acc = jnp.dot((oa_ref[0] * _silu(ga_ref[0])).astype(BF16), wout_ref[0:256, :], preferred_element_type=F32)
    acc = acc + jnp.dot(ob_ref[0].astype(BF16), wout_ref[256:512, :], preferred_element_type=F32)
    acc = acc + jnp.dot((o_c * _silu(gc_ref[0])).astype(BF16), wout_ref[512:768, :], preferred_element_type=F32)
    acc = acc + jnp.dot((o_d * _silu(gd_ref[0])).astype(BF16), wout_ref[768:1024, :], preferred_element_type=F32)
    x_new = x_ref[0] + acc
    if final:
        ms = jnp.mean(x_new * x_new, axis=-1, keepdims=True)
        x_new = (x_new * lax.rsqrt(ms + RMS_EPS)) * fg_ref[...]
    xo_ref[0] = x_new

    pool1_ref[0] = pext_ref[t_last + 1:t_last + pad_p, :]
    conv1_ref[0] = cext_ref[t_last + 2:t_last + pad_c, :]


def _mix(x, oa, ga, ob, uc, gc, ud, gd, pool0, conv0, w, final_g, batch, seq, tm, t_last, pos0, final):
    def tok(t, width):
        return t.reshape(batch, seq, width)

    def tok_spec(width):
        return pl.BlockSpec((1, tm, width), lambda b, i: (b, i, 0))

    def full(shape):
        return pl.BlockSpec(shape, lambda b, i: (0,) * len(shape))

    pool_spec = pl.BlockSpec((1, POOL_BUF, D_BRANCH), lambda b, i: (b, 0, 0))
    conv_spec = pl.BlockSpec((1, CONV_BUF, D_BRANCH), lambda b, i: (b, 0, 0))
    kern = functools.partial(_mix_kernel, tm=tm, t_last=t_last, pos0=pos0, final=final)
    return pl.pallas_call(
        kern,
        grid=(batch, seq // tm),
        in_specs=[tok_spec(D_MODEL), tok_spec(256), tok_spec(256), tok_spec(256), tok_spec(256), tok_spec(256),
                  tok_spec(512), tok_spec(256), pool_spec, conv_spec,
                  full((D_BRANCH, D_BRANCH)), full((1, D_BRANCH)), full((CONV_WIDTH, D_BRANCH)),
                  full((1, D_BRANCH)), full((1, D_BRANCH)), full((1, D_BRANCH)),
                  full((D_BRANCH, D_BRANCH)), full((D_MODEL, D_MODEL)), full((1, D_MODEL))],
        out_specs=[tok_spec(D_MODEL), pool_spec, conv_spec],
        out_shape=[jax.ShapeDtypeStruct((batch, seq, D_MODEL), F32),
                   jax.ShapeDtypeStruct((batch, POOL_BUF, D_BRANCH), F32),
                   jax.ShapeDtypeStruct((batch, CONV_BUF, D_BRANCH), F32)],
        scratch_shapes=[pltpu.VMEM((POOL_BUF + 1 + tm, D_BRANCH), F32),
                        pltpu.VMEM((CONV_BUF + 2 + tm, D_BRANCH), F32),
                        pltpu.VMEM((8, tm + 8, D_BRANCH), F32),
                        pltpu.VMEM((8, tm + 8 * ((CONV_WIDTH - 1) // 8), D_BRANCH), F32)],
        compiler_params=_params("parallel", "arbitrary"),
        name="mix",
    )(tok(x, D_MODEL), tok(oa, 256), tok(ga, 256), tok(ob, 256), tok(uc, 256), tok(gc, 256), tok(ud, 512),
      tok(gd, 256), pool0, conv0, w["pool_w"], w["pool_scale"], w["conv_w"], w["conv_b"], w["ln_g"], w["ln_b"],
      w["pw_out"], w["w_out"], final_g)


def _layer_weights(l, norm_g, w_in, fox_bf, rw_mu, rw_w0, rw_wup, rw_a0, rw_aup, rw_kk, rw_ka, rw_rk,
                   rw_gn_g, rw_gn_b, pool_w, pool_scale, conv_w, conv_b, ln_g, ln_b, pw_out, w_out):
    wi = w_in[l]
    o_f = 3 * D_BRANCH
    o_ga = o_f + N_HEADS
    main = jnp.concatenate([wi[:, 0:D_BRANCH], wi[:, o_ga:]], axis=1).astype(BF16)
    wkv_t = wi[:, D_BRANCH:o_f].T.astype(BF16)
    wf = jnp.zeros((8, D_MODEL), F32).at[0:N_HEADS].set(wi[:, o_f:o_ga].T).astype(BF16)
    bf = jnp.zeros((8, 1), F32).at[0:N_HEADS, 0].set(fox_bf[l])
    lora = jnp.zeros((2 * LORA, 2 * D_BRANCH), F32)
    lora = lora.at[0:LORA, 0:D_BRANCH].set(rw_wup[l]).at[LORA:, D_BRANCH:].set(rw_aup[l]).astype(BF16)
    pw_bd = jnp.zeros((D_BRANCH, D_BRANCH), F32)
    for g in range(len(POOL_WINDOWS)):
        gs = slice(g * HEAD_DIM, (g + 1) * HEAD_DIM)
        pw_bd = pw_bd.at[gs, gs].set(pool_w[l, g])

    def row(t):
        return t[l].reshape(1, -1)

    return dict(
        norm_g=row(norm_g), main=main, wkv_t=wkv_t, wf=wf, bf=bf,
        rwkv=dict(mu=row(rw_mu), w0=row(rw_w0), a0=row(rw_a0), lora=lora, kk=row(rw_kk), ka=row(rw_ka),
                  rk=row(rw_rk), gn_g=row(rw_gn_g), gn_b=row(rw_gn_b)),
        mix=dict(pool_w=pw_bd.astype(BF16), pool_scale=row(pool_scale), conv_w=conv_w[l], conv_b=row(conv_b),
                 ln_g=row(ln_g), ln_b=row(ln_b), pw_out=pw_out[l].astype(BF16), w_out=w_out[l].astype(BF16)))


def _stream_layer(x, lw, attend, wkv0, shift0, pool0, conv0, tiles, t_valid, pos0, flat_inproj, final_g, final):
    batch, seq, _ = x.shape
    x_in = x.reshape(1, batch * seq, D_MODEL) if flat_inproj else x
    *tok, kt, vt, lf_t = _inproj(x_in, lw["norm_g"], lw["main"], lw["wkv_t"], lw["wf"], lw["bf"], tiles["inproj"])
    q, ga, sh, gb, uc, gc, ud, gd = (t.reshape(batch, seq, t.shape[-1]) for t in tok)
    oa = attend(q, kt, vt, lf_t)
    if flat_inproj:
        prev0 = jnp.pad(shift0[:, None, :], ((0, 0), (0, seq - 1), (0, 0))).reshape(1, batch * seq, SHIFT_W)
        ob, wkv1 = _rwkv(sh.reshape(1, batch * seq, SHIFT_W), gb.reshape(1, batch * seq, D_BRANCH), wkv0, prev0,
                         lw["rwkv"], tiles["rwkv"], seq, t_valid, False)
        ob = ob.reshape(batch, seq, D_BRANCH)
    else:
        ob, wkv1 = _rwkv(sh, gb, wkv0, shift0[:, None, :], lw["rwkv"], tiles["rwkv"], tiles["chunk"],
                         tiles["chunk"], True)
    t_last = t_valid - (seq - tiles["mix"])
    x1, pool1, conv1 = _mix(x, oa, ga, ob, uc, gc, ud, gd, pool0, conv0, lw["mix"], final_g, batch, seq,
                            tiles["mix"], t_last, pos0, final)
    return x1, kt, vt, sh, lf_t, wkv1, pool1, conv1


def kernel(x_prompt, x_sample, cache_k, cache_v, cache_logf, state_wkv, state_shift, state_pool, state_conv,
           page_table, norm_g, w_in, fox_bf, rw_mu, rw_w0, rw_wup, rw_a0, rw_aup, rw_kk, rw_ka, rw_rk,
           rw_gn_g, rw_gn_b, pool_w, pool_scale, conv_w, conv_b, ln_g, ln_b, pw_out, w_out, final_norm_g):
    bp, tp, _ = x_prompt.shape
    bs, ts, _ = x_sample.shape
    depth = w_in.shape[0]
    n_phys = cache_k.shape[0]
    past = page_table.shape[1] * PAGE
    weights = (norm_g, w_in, fox_bf, rw_mu, rw_w0, rw_wup, rw_a0, rw_aup, rw_kk, rw_ka, rw_rk, rw_gn_g, rw_gn_b,
               pool_w, pool_scale, conv_w, conv_b, ln_g, ln_b, pw_out, w_out)

    tile_p = min(256, tp)
    tiles_p = dict(inproj=min(512, tp), rwkv=min(512, tp), chunk=min(64, tile_p), mix=tile_p)
    tiles_s = dict(inproj=min(256, bs * SAMPLE_PAD), rwkv=min(64, bs * SAMPLE_PAD), chunk=SAMPLE_PAD, mix=SAMPLE_PAD)

    xp = x_prompt
    xs = jnp.pad(x_sample, ((0, 0), (0, SAMPLE_PAD - ts), (0, 0)))
    cache_kt = jnp.transpose(cache_k, (0, 1, 3, 4, 2))
    cache_vt = jnp.transpose(cache_v, (0, 1, 3, 4, 2))
    cache_lft = jnp.transpose(cache_logf, (0, 1, 3, 2))

    wkv_zero = jnp.zeros((bp, N_HEADS, HEAD_DIM, HEAD_DIM), F32)
    shift_zero = jnp.zeros((bp, SHIFT_W), F32)
    pool_zero = jnp.zeros((bp, POOL_BUF, D_BRANCH), F32)
    conv_zero = jnp.zeros((bp, CONV_BUF, D_BRANCH), F32)

    def sample_heads(t):
        return t.reshape(N_HEADS, HEAD_DIM, bs, SAMPLE_PAD).transpose(2, 0, 1, 3)

    def pad_keys(t):
        return jnp.pad(t, [(0, 0)] * (t.ndim - 1) + [(0, PAGE - SAMPLE_PAD)])

    fg = final_norm_g.reshape(1, D_MODEL)
    outs_p, outs_s = [], []
    for l in range(depth):
        lw = _layer_weights(l, *weights)

        attend_p = lambda q, kt, vt, lf_t: _fox_prompt(q, kt, vt, lf_t, min(512, tp))
        xp, kt_p, vt_p, sh_p, lf_p, wkv_p, pool_p, conv_p = _stream_layer(
            xp, lw, attend_p, wkv_zero, shift_zero, pool_zero, conv_zero, tiles_p, tp, 0, False, fg, l == depth - 1)
        outs_p.append((kt_p.reshape(bp, N_HEADS, HEAD_DIM, tp), vt_p.reshape(bp, N_HEADS, HEAD_DIM, tp),
                       lf_p[:, 0:N_HEADS], wkv_p, sh_p[:, tp - 1], pool_p, conv_p))

        def attend_s(q, kt, vt, lf_t, l=l):
            lf_new = lf_t[0, 0:N_HEADS].reshape(N_HEADS, bs, SAMPLE_PAD).transpose(1, 0, 2)
            return _fox_sample(q, pad_keys(sample_heads(kt)), pad_keys(sample_heads(vt)), pad_keys(lf_new),
                               cache_kt, cache_vt, cache_lft, page_table, l, ts)

        xs, kt_s, vt_s, sh_s, lf_s, wkv_s, pool_s, conv_s = _stream_layer(
            xs, lw, attend_s, state_wkv[:, l], state_shift[:, l], state_pool[:, l], state_conv[:, l],
            tiles_s, ts, past, True, fg, l == depth - 1)
        outs_s.append((sample_heads(kt_s).transpose(0, 3, 1, 2)[:, 0:ts],
                       sample_heads(vt_s).transpose(0, 3, 1, 2)[:, 0:ts],
                       lf_s[0, 0:N_HEADS].reshape(N_HEADS, bs, SAMPLE_PAD).transpose(1, 2, 0)[:, 0:ts],
                       wkv_s, sh_s[:, ts - 1], pool_s, conv_s))

    y_prompt = xp
    y_sample = xs[:, 0:ts]

    def stk(outs, i):
        return jnp.stack([o[i] for o in outs], axis=1)

    k_prompt = stk(outs_p, 0).transpose(0, 1, 4, 2, 3)
    v_prompt = stk(outs_p, 1).transpose(0, 1, 4, 2, 3)
    logf_prompt = stk(outs_p, 2).transpose(0, 1, 3, 2)
    return (y_prompt, y_sample, k_prompt, v_prompt, logf_prompt,
            *(stk(outs_p, i) for i in range(3, 7)),
            *(stk(outs_s, i) for i in range(7)))
```

```python
import functools

import jax
import jax.numpy as jnp
from jax import lax
from jax.experimental import pallas as pl
from jax.experimental.pallas import tpu as pltpu

F32 = jnp.float32
BF16 = jnp.bfloat16

D_MODEL = 1024
D_BRANCH = 256
HEAD_DIM = 64
N_HEADS = 4
LORA = 64
SHIFT_W = 3 * D_BRANCH + 2 * LORA
PAGE = 128
POOL_WINDOWS = (2, 4, 8, 16)
POOL_BUF = 15
CONV_WIDTH = 31
CONV_BUF = 30
RMS_EPS = 1e-6
LN_EPS = 1e-5
GN_EPS = 64e-5
ATTN_SCALE = HEAD_DIM ** -0.5
NEG_BIG = -1e30
SAMPLE_PAD = 8
PAGES_PER_STEP = 32
RWKV_PIECE = 64
RWKV_GROUP = 32
VMEM_LIMIT = 56 * 1024 * 1024

NN = ((1,), (0,))
NT = ((1,), (1,))
TN = ((0,), (0,))


def _dg(a, b, dims=NN):
    return lax.dot_general(a, b, (dims, ((), ())), preferred_element_type=F32)


def _split3(x):
    hi = x.astype(BF16)
    r1 = x - hi.astype(F32)
    mid = r1.astype(BF16)
    lo = (r1 - mid.astype(F32)).astype(BF16)
    return hi, mid, lo


def _dot_f32(a, b, dims=NN):
    ah, am, _ = _split3(a)
    bh, bm, _ = _split3(b)
    return _dg(ah, bh, dims) + (_dg(ah, bm, dims) + _dg(am, bh, dims))


def _dot_exact_rhs(a, b_bf16, dims=NN):
    ah, am, al = _split3(a)
    return _dg(ah, b_bf16, dims) + (_dg(am, b_bf16, dims) + _dg(al, b_bf16, dims))


def _dot_exact_lhs(a_bf16, b, dims=NN):
    bh, bm, bl = _split3(b)
    return _dg(a_bf16, bh, dims) + (_dg(a_bf16, bm, dims) + _dg(a_bf16, bl, dims))


def _group_sum(a, ones_bf16):
    hi = a.astype(BF16)
    mid = (a - hi.astype(F32)).astype(BF16)
    return _dg(hi, ones_bf16) + _dg(mid, ones_bf16)


def _sigmoid(x):
    return 1.0 / (1.0 + jnp.exp(-x))


def _silu(x):
    return x * _sigmoid(x)


def _softplus(x):
    return jnp.maximum(x, 0.0) + jnp.log1p(jnp.exp(-jnp.abs(x)))


def _iota(shape, dim):
    return lax.broadcasted_iota(jnp.int32, shape, dim)


def _params(*sem):
    return pltpu.CompilerParams(dimension_semantics=sem, vmem_limit_bytes=VMEM_LIMIT)


_IN_SEGS = (("q", 256), ("ga", 256), ("sh", SHIFT_W), ("gb", 256),
            ("uc", 256), ("gc", 256), ("ud", 512), ("gd", 256))
_IN_COLS = sum(s for _, s in _IN_SEGS)


def _inproj_kernel(x_ref, g_ref, w_ref, wkv_ref, wf_ref, bf_ref, *out_refs):
    x = x_ref[0]
    ms = jnp.mean(x * x, axis=-1, keepdims=True)
    h = ((x * lax.rsqrt(ms + RMS_EPS)) * g_ref[...]).astype(BF16)
    off = 0
    for (_, size), o_ref in zip(_IN_SEGS, out_refs[:-3]):
        o_ref[0] = jnp.dot(h, w_ref[:, off:off + size], preferred_element_type=F32)
        off += size
    kt_ref, vt_ref, lf_ref = out_refs[-3:]
    kt_ref[0] = _dg(wkv_ref[0:D_BRANCH, :], h, NT)
    vt_ref[0] = _dg(wkv_ref[D_BRANCH:2 * D_BRANCH, :], h, NT)
    f = _dg(wf_ref[...], h, NT) + bf_ref[...]
    lf_ref[0] = -_softplus(-f)


def _inproj(x, g, w, wkv_t, wf, bf, tm):
    batch, seq, _ = x.shape
    out_shape = [jax.ShapeDtypeStruct((batch, seq, s), F32) for _, s in _IN_SEGS]
    out_shape += [jax.ShapeDtypeStruct((batch, D_BRANCH, seq), F32)] * 2
    out_shape.append(jax.ShapeDtypeStruct((batch, 8, seq), F32))
    out_specs = [pl.BlockSpec((1, tm, s), lambda b, i: (b, i, 0)) for _, s in _IN_SEGS]
    out_specs += [pl.BlockSpec((1, D_BRANCH, tm), lambda b, i: (b, 0, i))] * 2
    out_specs.append(pl.BlockSpec((1, 8, tm), lambda b, i: (b, 0, i)))

    def full(shape):
        return pl.BlockSpec(shape, lambda b, i: (0,) * len(shape))

    return pl.pallas_call(
        _inproj_kernel,
        grid=(batch, seq // tm),
        in_specs=[pl.BlockSpec((1, tm, D_MODEL), lambda b, i: (b, i, 0)),
                  full((1, D_MODEL)), full((D_MODEL, _IN_COLS)), full((2 * D_BRANCH, D_MODEL)),
                  full((8, D_MODEL)), full((8, 1))],
        out_specs=out_specs,
        out_shape=out_shape,
        compiler_params=_params("parallel", "parallel"),
        name="inproj",
    )(x, g, w, wkv_t, wf, bf)


def _fox_prompt_kernel(q_ref, kt_ref, vt_ref, lf_ref, o_ref, c_ref, kb_ref, vb_ref, m_ref, acc_ref, *, tq, seq):
    i = pl.program_id(1)
    heads = [slice(h * HEAD_DIM, (h + 1) * HEAD_DIM) for h in range(N_HEADS)]

    @pl.when(i == 0)
    def _per_sequence():
        upper = (_iota((128, 128), 0) <= _iota((128, 128), 1)).astype(BF16)
        carry = jnp.zeros((8, 1), F32)
        for blk in range(seq // 128):
            cs = _dot_exact_rhs(lf_ref[0, :, blk * 128:(blk + 1) * 128], upper) + carry
            c_ref[:, blk * 128:(blk + 1) * 128] = cs
            carry = cs[:, 127:128]
        kb_ref[...] = kt_ref[0].astype(BF16)
        for h, hs in enumerate(heads):
            vb_ref[h, 0:HEAD_DIM, :] = vt_ref[0, hs, :].astype(BF16)
            vb_ref[h, HEAD_DIM:2 * HEAD_DIM, :] = jnp.ones((HEAD_DIM, seq), BF16)

    row0 = pl.multiple_of(i * tq, tq)
    c_tile = c_ref[:, pl.ds(row0, tq)]
    causal = _iota((tq, tq), 1) <= _iota((tq, tq), 0)

    q_all = q_ref[0] * ATTN_SCALE
    q_heads = [q_all[:, hs].astype(BF16) for hs in heads]
    m_ref[...] = jnp.full(m_ref.shape, NEG_BIG, F32)
    acc_ref[...] = jnp.zeros(acc_ref.shape, F32)

    def tile_update(col0, masked):
        for h, hs in enumerate(heads):
            kh = kb_ref[hs, pl.ds(col0, tq)]
            vh = vb_ref[h, :, pl.ds(col0, tq)]
            s = _dg(q_heads[h], kh) + (c_tile[h:h + 1, 0:1] - c_ref[h:h + 1, pl.ds(col0, tq)])
            if masked:
                s = jnp.where(causal, s, NEG_BIG)
            m_old = m_ref[h]
            m_new = jnp.maximum(m_old, jnp.max(s, axis=-1, keepdims=True))
            p = jnp.exp(s - jnp.tile(m_new, (1, tq // 128)))
            alpha = jnp.exp(m_old - m_new)
            acc_ref[h] = alpha * acc_ref[h] + _dg(p.astype(BF16), vh, NT)
            m_ref[h] = m_new

    def body(j, carry):
        tile_update(pl.multiple_of(j * tq, tq), False)
        return carry

    lax.fori_loop(0, i, body, 0)
    tile_update(row0, True)
    for h, hs in enumerate(heads):
        a = acc_ref[h]
        o_ref[0, :, hs] = a[:, 0:HEAD_DIM] / a[:, HEAD_DIM:2 * HEAD_DIM]


def _fox_prompt(q, kt, vt, lf_t, tq):
    batch, seq, _ = q.shape
    kern = functools.partial(_fox_prompt_kernel, tq=tq, seq=seq)
    return pl.pallas_call(
        kern,
        grid=(batch, seq // tq),
        in_specs=[pl.BlockSpec((1, tq, D_BRANCH), lambda b, i: (b, i, 0)),
                  pl.BlockSpec((1, D_BRANCH, seq), lambda b, i: (b, 0, 0)),
                  pl.BlockSpec((1, D_BRANCH, seq), lambda b, i: (b, 0, 0)),
                  pl.BlockSpec((1, 8, seq), lambda b, i: (b, 0, 0))],
        out_specs=pl.BlockSpec((1, tq, D_BRANCH), lambda b, i: (b, i, 0)),
        out_shape=jax.ShapeDtypeStruct((batch, seq, D_BRANCH), F32),
        scratch_shapes=[pltpu.VMEM((8, seq), F32),
                        pltpu.VMEM((D_BRANCH, seq), BF16), pltpu.VMEM((N_HEADS, 2 * HEAD_DIM, seq), BF16),
                        pltpu.VMEM((N_HEADS, tq, 128), F32), pltpu.VMEM((N_HEADS, tq, 2 * HEAD_DIM), F32)],
        compiler_params=_params("parallel", "arbitrary"),
        name="fox_prompt",
    )(q, kt, vt, lf_t)


def _fox_sample_kernel(pt_ref, q_ref, kn_ref, vn_ref, lfn_ref, ck_ref, cv_ref, clf_ref, o_ref,
                       kbuf, vbuf, lfbuf, suffix_ref, total_ref, s_ref, ksem, vsem, lfsem,
                       *, layer, n_pages, t_valid):
    b = pl.program_id(0)
    slot = b % 2
    rows = N_HEADS * SAMPLE_PAD
    group = min(PAGES_PER_STEP, n_pages)

    def page_copies(seq, slot_, p):
        page = pt_ref[seq, p]
        return (pltpu.make_async_copy(ck_ref.at[page, layer], kbuf.at[slot_, p], ksem.at[slot_]),
                pltpu.make_async_copy(cv_ref.at[page, layer], vbuf.at[slot_, p], vsem.at[slot_]),
                pltpu.make_async_copy(clf_ref.at[page, layer], lfbuf.at[slot_, p, 0:N_HEADS], lfsem.at[slot_]))

    def start_fetch(seq, slot_):
        def body(p, carry):
            for cp in page_copies(seq, slot_, p):
                cp.start()
            return carry
        lax.fori_loop(0, n_pages, body, 0)

    def wait_fetch(seq, slot_):
        def body(p, carry):
            for cp in page_copies(seq, slot_, p):
                cp.wait()
            return carry
        lax.fori_loop(0, n_pages, body, 0)

    @pl.when(b == 0)
    def _first():
        lfbuf[...] = jnp.zeros(lfbuf.shape, F32)
        start_fetch(0, 0)

    @pl.when(b + 1 < pl.num_programs(0))
    def _prefetch_next():
        start_fetch(b + 1, 1 - slot)

    wait_fetch(b, slot)

    q_all = q_ref[0] * ATTN_SCALE
    q_heads = [q_all[:, h * HEAD_DIM:(h + 1) * HEAD_DIM].astype(BF16) for h in range(N_HEADS)]

    key4 = _iota((N_HEADS, PAGE), 1)
    lfbuf[slot, n_pages, 0:N_HEADS, :] = jnp.where(key4 < t_valid, lfn_ref[0], 0.0)
    lf_all = lfbuf[slot].reshape((n_pages + 1) * 8, PAGE)
    later = (_iota((PAGE, PAGE), 0) > _iota((PAGE, PAGE), 1)).astype(BF16)
    suffix_ref[...] = _dot_exact_rhs(lf_all, later)
    total_ref[...] = _dot_exact_rhs(lf_all, jnp.ones((PAGE, PAGE), BF16))

    def scores(kt, bias8, col0, mask):
        for h in range(N_HEADS):
            s = _dg(q_heads[h], kt[h].astype(BF16)) + bias8[h:h + 1, :]
            if mask is not None:
                s = jnp.where(mask, s, NEG_BIG)
            s_ref[h * SAMPLE_PAD:(h + 1) * SAMPLE_PAD, pl.ds(col0, PAGE)] = s

    new0 = n_pages * 8
    key = _iota((SAMPLE_PAD, PAGE), 1)
    qry = _iota((SAMPLE_PAD, PAGE), 0)
    scores(kn_ref[0], suffix_ref[new0:new0 + 8, :], n_pages * PAGE,
           jnp.logical_and(key <= qry, key < t_valid))

    def score_group(g, running):
        lo = n_pages - (g + 1) * group
        bias = [None] * group
        for i in reversed(range(group)):
            r0 = pl.multiple_of((lo + i) * 8, 8)
            bias[i] = suffix_ref[pl.ds(r0, 8), :] + running
            running = running + total_ref[pl.ds(r0, 8), :]
        col0 = pl.multiple_of(lo * PAGE, group * PAGE)
        for h in range(N_HEADS):
            kt_wide = jnp.concatenate([kbuf[slot, lo + i, h].astype(BF16) for i in range(group)], axis=1)
            bias_h = jnp.concatenate([b8[h:h + 1, :] for b8 in bias], axis=1)
            s_ref[h * SAMPLE_PAD:(h + 1) * SAMPLE_PAD, pl.ds(col0, group * PAGE)] = _dg(q_heads[h], kt_wide) + bias_h
        return running

    lax.fori_loop(0, n_pages // group, score_group, total_ref[new0:new0 + 8, :])

    s = s_ref[...]
    m = jnp.max(s, axis=-1, keepdims=True)
    p_all = jnp.exp(s - m)
    denom = jnp.sum(p_all, axis=-1, keepdims=True)
    s_ref[...] = p_all

    def weighted(vt, col0):
        return jnp.concatenate(
            [_dg(s_ref[h * SAMPLE_PAD:(h + 1) * SAMPLE_PAD, pl.ds(col0, PAGE)].astype(BF16),
                 vt[h].astype(BF16), NT) for h in range(N_HEADS)], axis=0)

    def value_group(g, acc):
        lo = g * group
        col0 = pl.multiple_of(lo * PAGE, group * PAGE)
        parts = []
        for h in range(N_HEADS):
            vt_wide = jnp.concatenate([vbuf[slot, lo + i, h].astype(BF16) for i in range(group)], axis=1)
            p_wide = s_ref[h * SAMPLE_PAD:(h + 1) * SAMPLE_PAD, pl.ds(col0, group * PAGE)].astype(BF16)
            parts.append(_dg(p_wide, vt_wide, NT))
        return acc + jnp.concatenate(parts, axis=0)

    acc = lax.fori_loop(0, n_pages // group, value_group, weighted(vn_ref[0], n_pages * PAGE))
    out = acc / denom
    for h in range(N_HEADS):
        o_ref[0, :, h * HEAD_DIM:(h + 1) * HEAD_DIM] = out[h * SAMPLE_PAD:(h + 1) * SAMPLE_PAD]


def _fox_sample(q, kt_new, vt_new, lf_new, cache_kt, cache_vt, cache_lft, page_table, layer, t_valid):
    batch, n_pages = page_table.shape
    rows = N_HEADS * SAMPLE_PAD
    assert n_pages % min(PAGES_PER_STEP, n_pages) == 0

    def per_seq(shape):
        return pl.BlockSpec((1,) + shape, lambda b, pt: (b,) + (0,) * len(shape))

    in_specs = [per_seq((SAMPLE_PAD, D_BRANCH)), per_seq((N_HEADS, HEAD_DIM, PAGE)),
                per_seq((N_HEADS, HEAD_DIM, PAGE)), per_seq((N_HEADS, PAGE)),
                pl.BlockSpec(memory_space=pl.ANY), pl.BlockSpec(memory_space=pl.ANY),
                pl.BlockSpec(memory_space=pl.ANY)]
    kern = functools.partial(_fox_sample_kernel, layer=layer, n_pages=n_pages, t_valid=t_valid)
    return pl.pallas_call(
        kern,
        grid_spec=pltpu.PrefetchScalarGridSpec(
            num_scalar_prefetch=1,
            grid=(batch,),
            in_specs=in_specs,
            out_specs=per_seq((SAMPLE_PAD, D_BRANCH)),
            scratch_shapes=[pltpu.VMEM((2, n_pages, N_HEADS, HEAD_DIM, PAGE), F32),
                            pltpu.VMEM((2, n_pages, N_HEADS, HEAD_DIM, PAGE), F32),
                            pltpu.VMEM((2, n_pages + 1, 8, PAGE), F32),
                            pltpu.VMEM(((n_pages + 1) * 8, PAGE), F32),
                            pltpu.VMEM(((n_pages + 1) * 8, PAGE), F32),
                            pltpu.VMEM((rows, (n_pages + 1) * PAGE), F32),
                            pltpu.SemaphoreType.DMA((2,)), pltpu.SemaphoreType.DMA((2,)),
                            pltpu.SemaphoreType.DMA((2,))]),
        out_shape=jax.ShapeDtypeStruct((batch, SAMPLE_PAD, D_BRANCH), F32),
        compiler_params=_params("arbitrary"),
        name="fox_sample",
    )(page_table, q, kt_new, vt_new, lf_new, cache_kt, cache_vt, cache_lft)


def _unit_lower_inverses(ns, size, chunk):
    r = _iota((size, size), 0)
    c = _iota((size, size), 1)
    eye = (r == c).astype(F32)
    pair = (r // 2) == (c // 2)
    ms = [eye - jnp.where(pair, n.astype(F32), 0.0) for n in ns]
    blk = 2
    while blk < chunk:
        off = jnp.logical_and((r // (2 * blk)) == (c // (2 * blk)), (r // blk) != (c // blk))
        mbs = [m.astype(BF16) for m in ms]
        ts = [_dg(mb, n).astype(BF16) for mb, n in zip(mbs, ns)]
        ts = [_dg(t, mb) for t, mb in zip(ts, mbs)]
        ms = [m - jnp.where(off, t, 0.0) for m, t in zip(ms, ts)]
        blk *= 2
    return ms


def _rwkv_kernel(sh_ref, gb_ref, s0_ref, prev0_ref, mu_ref, w0_ref, a0_ref, lora_ref, kkw_ref, ka_ref,
                 rk_ref, gng_ref, gnb_ref, o_ref, s1_ref, ext_ref, st_ref, oacc_ref, *, tb, chunk, t_valid,
                 carried):
    n_blk = tb // chunk
    sh = sh_ref[0]
    if carried:
        i = pl.program_id(1)

        @pl.when(i == 0)
        def _init():
            st_ref[...] = s0_ref[0]
            ext_ref[0:8, :] = jnp.zeros((8, SHIFT_W), F32)
            ext_ref[7:8, :] = prev0_ref[0]

        @pl.when(i > 0)
        def _carry_shift():
            ext_ref[7:8, :] = ext_ref[tb + 7:tb + 8, :]

        ext_ref[8:8 + tb, :] = sh
        prev = ext_ref[7:7 + tb, :]
    else:
        ext_ref[0:8, :] = jnp.zeros((8, SHIFT_W), F32)
        ext_ref[8:8 + tb, :] = sh
        first = (_iota((tb, 1), 0) % chunk) == 0
        prev = jnp.where(first, prev0_ref[0], ext_ref[7:7 + tb, :])
    xm = sh + mu_ref[...] * (prev - sh)
    r = xm[:, 0:256]
    k = xm[:, 256:512]
    v = xm[:, 512:768]
    wa = xm[:, 768:896]
    wa = jnp.where(_iota((tb, 2 * LORA), 1) < LORA, jnp.tanh(wa), wa)
    lora = jnp.dot(wa.astype(BF16), lora_ref[...], preferred_element_type=F32)
    w_log = -_softplus(-(w0_ref[...] + lora[:, 0:256])) - 0.5
    log_decay = -jnp.exp(w_log)
    a = _sigmoid(a0_ref[...] + lora[:, 256:512])

    grp = ((_iota((D_BRANCH, D_BRANCH), 0) // HEAD_DIM) == (_iota((D_BRANCH, D_BRANCH), 1) // HEAD_DIM)).astype(BF16)
    kk = k * kkw_ref[...]
    kk = kk * lax.rsqrt(jnp.maximum(_group_sum(kk * kk, grp), 1e-12))
    kp = k * (1.0 + (a - 1.0) * ka_ref[...])
    beta = kk * a
    bonus = _group_sum(r * kp * rk_ref[...], grp) * v
    if t_valid < chunk:
        keep = ((_iota((tb, 1), 0) % chunk) < t_valid).astype(F32)
        log_decay = log_decay * keep
        kp = kp * keep
        beta = beta * keep

    rr = _iota((tb, tb), 0)
    cc = _iota((tb, tb), 1)
    incl = jnp.logical_and((rr // chunk) == (cc // chunk), rr >= cc)

    lc = _dot_exact_lhs(incl.astype(BF16), log_decay)
    lc_tot = jnp.concatenate(
        [jnp.broadcast_to(lc[(c + 1) * chunk - 1:(c + 1) * chunk, :], (chunk, D_BRANCH)) for c in range(n_blk)],
        axis=0)
    to_end = jnp.exp(lc_tot - lc)
    g_inv = jnp.exp(-lc)
    g_end = jnp.exp(lc_tot)
    qc = kk * jnp.exp(lc - log_decay)
    rc = r * jnp.exp(lc)
    kc = kp * g_inv
    bc = beta * g_inv
    kd = kp * to_end
    bd = beta * to_end
    eye_k = (_iota((HEAD_DIM, HEAD_DIM), 0) == _iota((HEAD_DIM, HEAD_DIM), 1)).astype(F32)

    def b16(t):
        return t.astype(BF16)

    mat = min(tb, RWKV_PIECE)
    n_sub = tb // mat
    rg = _iota((2 * mat, 2 * mat), 0)
    rl = rg % mat
    cl = _iota((2 * mat, 2 * mat), 1) % mat
    g_mask = jnp.logical_and((rl // chunk) == (cl // chunk), rl + rg // mat > cl)
    pieces = [(sub, h) for sub in range(n_sub) for h in range(N_HEADS)]

    def piece(t, sub, h):
        return t[sub * mat:(sub + 1) * mat, h * HEAD_DIM:(h + 1) * HEAD_DIM]

    zero_b = jnp.zeros((mat, HEAD_DIM), BF16)
    r_f, v_f, v_b, kd_f, bd_f, w_f, y_f, o0, rp = ([] for _ in range(9))
    for g0 in range(0, len(pieces), RWKV_GROUP):
        grp_pieces = pieces[g0:g0 + RWKV_GROUP]
        r_g = [piece(rc, *ph) for ph in grp_pieces]
        v_g = [piece(v, *ph) for ph in grp_pieces]
        q_b = [b16(piece(qc, *ph)) for ph in grp_pieces]
        k_b = [b16(piece(kc, *ph)) for ph in grp_pieces]
        b_b = [b16(piece(bc, *ph)) for ph in grp_pieces]
        r_b = [b16(t) for t in r_g]
        vb_g = [b16(t) for t in v_g]
        g_all = [b16(jnp.where(g_mask, _dg(jnp.concatenate([q, rr_], axis=0), jnp.concatenate([bb, k], axis=0), NT),
                               0.0)) for q, rr_, bb, k in zip(q_b, r_b, b_b, k_b)]
        g_top = [g[0:mat] for g in g_all]
        g_bot = [g[mat:2 * mat] for g in g_all]
        inv = [b16(t) for t in _unit_lower_inverses([g[:, 0:mat] for g in g_top], mat, chunk)]
        w_g = [_dg(m_, q) for m_, q in zip(inv, q_b)]
        av_b = [b16(_dg(g, jnp.concatenate([zero_b, vv], axis=0))) for g, vv in zip(g_top, vb_g)]
        y_g = [_dg(m_, av) for m_, av in zip(inv, av_b)]
        o0 += [_dg(g, jnp.concatenate([b16(-yy), vv], axis=0)) for g, yy, vv in zip(g_bot, y_g, vb_g)]
        rp += [rr_ - _dg(g, jnp.concatenate([b16(ww), zero_b], axis=0)) for rr_, g, ww in zip(r_g, g_bot, w_g)]
        r_f += r_g
        v_f += v_g
        v_b += vb_g
        w_f += w_g
        y_f += y_g
        kd_f += [piece(kd, *ph) for ph in grp_pieces]
        bd_f += [piece(bd, *ph) for ph in grp_pieces]

    blocks_per_sub = mat // chunk
    for h in range(N_HEADS):
        hs = slice(h * HEAD_DIM, (h + 1) * HEAD_DIM)
        state = st_ref[h] if carried else None
        for c in range(n_blk):
            sub, loc = divmod(c, blocks_per_sub)
            idx = sub * N_HEADS + h
            ls = slice(loc * chunk, (loc + 1) * chunk)
            sl = slice(c * chunk, (c + 1) * chunk)
            bd_c = b16(bd_f[idx][ls])
            p_mat = eye_k * g_end[c * chunk:c * chunk + 1, hs] - _dg(b16(w_f[idx][ls]), bd_c, TN)
            z_mat = _dg(v_b[idx][ls] if chunk % 16 == 0 else b16(v_f[idx][ls]), b16(kd_f[idx][ls]), TN) \
                - _dg(b16(y_f[idx][ls]), bd_c, TN)
            s_in = state if carried else s0_ref[c, h]
            oacc_ref[sl, hs] = o0[idx][ls] + _dg(b16(rp[idx][ls]), b16(s_in), NT)
            s_out = _dot_f32(s_in, p_mat) + z_mat
            if carried:
                state = s_out
            else:
                s1_ref[c, h] = s_out
        if carried:
            st_ref[h] = state

    o = oacc_ref[...]
    mean = _group_sum(o, grp) * (1.0 / HEAD_DIM)
    d = o - mean
    var = _group_sum(d * d, grp) * (1.0 / HEAD_DIM)
    o = d * lax.rsqrt(var + GN_EPS) * gng_ref[...] + gnb_ref[...] + bonus
    o_ref[0] = o * _silu(gb_ref[0])
    if carried:
        s1_ref[0] = st_ref[...]


def _rwkv(sh, gb, s0, prev0, w, tb, chunk, t_valid, carried):
    batch, seq, _ = sh.shape
    kern = functools.partial(_rwkv_kernel, tb=tb, chunk=chunk, t_valid=t_valid, carried=carried)

    def row(width):
        return pl.BlockSpec((1, width), lambda b, i: (0, 0))

    if carried:
        state_spec = pl.BlockSpec((1, N_HEADS, HEAD_DIM, HEAD_DIM), lambda b, i: (b, 0, 0, 0))
        prev_spec = pl.BlockSpec((1, 1, SHIFT_W), lambda b, i: (b, 0, 0))
    else:
        state_spec = pl.BlockSpec((tb // chunk, N_HEADS, HEAD_DIM, HEAD_DIM), lambda b, i: (i, 0, 0, 0))
        prev_spec = pl.BlockSpec((1, tb, SHIFT_W), lambda b, i: (b, i, 0))
    return pl.pallas_call(
        kern,
        grid=(batch, seq // tb),
        in_specs=[pl.BlockSpec((1, tb, SHIFT_W), lambda b, i: (b, i, 0)),
                  pl.BlockSpec((1, tb, D_BRANCH), lambda b, i: (b, i, 0)),
                  state_spec, prev_spec,
                  row(SHIFT_W), row(D_BRANCH), row(D_BRANCH),
                  pl.BlockSpec((2 * LORA, 2 * D_BRANCH), lambda b, i: (0, 0)),
                  row(D_BRANCH), row(D_BRANCH), row(D_BRANCH), row(D_BRANCH), row(D_BRANCH)],
        out_specs=[pl.BlockSpec((1, tb, D_BRANCH), lambda b, i: (b, i, 0)), state_spec],
        out_shape=[jax.ShapeDtypeStruct((batch, seq, D_BRANCH), F32),
                   jax.ShapeDtypeStruct(s0.shape, F32)],
        scratch_shapes=[pltpu.VMEM((tb + 8, SHIFT_W), F32),
                        pltpu.VMEM((N_HEADS, HEAD_DIM, HEAD_DIM), F32),
                        pltpu.VMEM((tb, D_BRANCH), F32)],
        compiler_params=_params("parallel", "arbitrary"),
        name="rwkv",
    )(sh, gb, s0, prev0, w["mu"], w["w0"], w["a0"], w["lora"], w["kk"], w["ka"],
      w["rk"], w["gn_g"], w["gn_b"])


def _mix_kernel(x_ref, oa_ref, ga_ref, ob_ref, uc_ref, gc_ref, ud_ref, gd_ref, pool0_ref, conv0_ref,
                poolw_ref, pscale_ref, convw_ref, convb_ref, lng_ref, lnb_ref, pw_ref, wout_ref, fg_ref,
                xo_ref, pool1_ref, conv1_ref, pext_ref, cext_ref, pshift_ref, cshift_ref,
                *, tm, t_last, pos0, final):
    i = pl.program_id(1)
    pad_p = POOL_BUF + 1
    pad_c = CONV_BUF + 2

    @pl.when(i == 0)
    def _init():
        pext_ref[0:pad_p, :] = jnp.zeros((pad_p, D_BRANCH), F32)
        cext_ref[0:pad_c, :] = jnp.zeros((pad_c, D_BRANCH), F32)
        pext_ref[1:pad_p, :] = pool0_ref[0]
        cext_ref[2:pad_c, :] = conv0_ref[0]

    @pl.when(i > 0)
    def _carry():
        pext_ref[1:pad_p, :] = pext_ref[tm + 1:tm + pad_p, :]
        cext_ref[2:pad_c, :] = cext_ref[tm + 2:tm + pad_c, :]

    uc = uc_ref[0]
    pext_ref[pad_p:pad_p + tm, :] = uc
    ud = ud_ref[0]
    cext_ref[pad_c:pad_c + tm, :] = ud[:, 0:D_BRANCH] * _sigmoid(ud[:, D_BRANCH:2 * D_BRANCH])

    done_p = set()

    def back(d):
        a, sub = divmod(pad_p - d, 8)
        if sub not in done_p:
            done_p.add(sub)
            pshift_ref[sub] = pext_ref[sub:sub + tm + 8, :]
        return pshift_ref[sub, 8 * a:8 * a + tm, :]

    win2 = uc + back(1)
    win4 = win2 + back(2) + back(3)
    win8 = win4
    for d in range(4, 8):
        win8 = win8 + back(d)
    win16 = win8
    for d in range(8, 16):
        win16 = win16 + back(d)
    group = _iota((tm, D_BRANCH), 1) // HEAD_DIM
    win = jnp.where(group == 0, win2, jnp.where(group == 1, win4, jnp.where(group == 2, win8, win16)))
    width = jnp.where(group == 0, 2, jnp.where(group == 1, 4, jnp.where(group == 2, 8, 16)))
    pos = pos0 + i * tm + _iota((tm, D_BRANCH), 0)
    cnt = jnp.minimum(pos + 1, width).astype(F32)
    pooled = win / cnt - uc
    o_c = jnp.dot(pooled.astype(BF16), poolw_ref[...], preferred_element_type=F32) * pscale_ref[...]

    y = jnp.zeros((tm, D_BRANCH), F32) + convb_ref[...]
    for sub in range(8):
        taps = range(sub, CONV_WIDTH, 8)
        span = tm + 8 * (len(taps) - 1)
        cshift_ref[sub, 0:span, :] = cext_ref[2 + sub:2 + sub + span, :]
        for a, jt in enumerate(taps):
            y = y + cshift_ref[sub, 8 * a:8 * a + tm, :] * convw_ref[jt:jt + 1, :]
    mean = jnp.mean(y, axis=-1, keepdims=True)
    yc = y - mean
    var = jnp.mean(yc * yc, axis=-1, keepdims=True)
    yn = yc * lax.rsqrt(var + LN_EPS) * lng_ref[...] + lnb_ref[...]
    o_d = jnp.dot(_silu(yn).astype(BF16), pw_ref[...], preferred_element_type=F32)

    acc = jnp.dot((oa_ref[0] * _silu(ga_ref[0])).astype(BF16), wout_ref[0:256, :], preferred_element_type=F32)
    acc = acc + jnp.dot(ob_ref[0].astype(BF16), wout_ref[256:512, :], preferred_element_type=F32)
    acc = acc + jnp.dot((o_c * _silu(gc_ref[0])).astype(BF16), wout_ref[512:768, :], preferred_element_type=F32)
    acc = acc + jnp.dot((o_d * _silu(gd_ref[0])).astype(BF16), wout_ref[768:1024, :], preferred_element_type=F32)
    x_new = x_ref[0] + acc
    if final:
        ms = jnp.mean(x_new * x_new, axis=-1, keepdims=True)
        x_new = (x_new * lax.rsqrt(ms + RMS_EPS)) * fg_ref[...]
    xo_ref[0] = x_new

    pool1_ref[0] = pext_ref[t_last + 1:t_last + pad_p, :]
    conv1_ref[0] = cext_ref[t_last + 2:t_last + pad_c, :]


def _mix(x, oa, ga, ob, uc, gc, ud, gd, pool0, conv0, w, final_g, batch, seq, tm, t_last, pos0, final):
    def tok(t, width):
        return t.reshape(batch, seq, width)

    def tok_spec(width):
        return pl.BlockSpec((1, tm, width), lambda b, i: (b, i, 0))

    def full(shape):
        return pl.BlockSpec(shape, lambda b, i: (0,) * len(shape))

    pool_spec = pl.BlockSpec((1, POOL_BUF, D_BRANCH), lambda b, i: (b, 0, 0))
    conv_spec = pl.BlockSpec((1, CONV_BUF, D_BRANCH), lambda b, i: (b, 0, 0))
    kern = functools.partial(_mix_kernel, tm=tm, t_last=t_last, pos0=pos0, final=final)
    return pl.pallas_call(
        kern,
        grid=(batch, seq // tm),
        in_specs=[tok_spec(D_MODEL), tok_spec(256), tok_spec(256), tok_spec(256), tok_spec(256), tok_spec(256),
                  tok_spec(512), tok_spec(256), pool_spec, conv_spec,
                  full((D_BRANCH, D_BRANCH)), full((1, D_BRANCH)), full((CONV_WIDTH, D_BRANCH)),
                  full((1, D_BRANCH)), full((1, D_BRANCH)), full((1, D_BRANCH)),
                  full((D_BRANCH, D_BRANCH)), full((D_MODEL, D_MODEL)), full((1, D_MODEL))],
        out_specs=[tok_spec(D_MODEL), pool_spec, conv_spec],
        out_shape=[jax.ShapeDtypeStruct((batch, seq, D_MODEL), F32),
                   jax.ShapeDtypeStruct((batch, POOL_BUF, D_BRANCH), F32),
                   jax.ShapeDtypeStruct((batch, CONV_BUF, D_BRANCH), F32)],
        scratch_shapes=[pltpu.VMEM((POOL_BUF + 1 + tm, D_BRANCH), F32),
                        pltpu.VMEM((CONV_BUF + 2 + tm, D_BRANCH), F32),
                        pltpu.VMEM((8, tm + 8, D_BRANCH), F32),
                        pltpu.VMEM((8, tm + 8 * ((CONV_WIDTH - 1) // 8), D_BRANCH), F32)],
        compiler_params=_params("parallel", "arbitrary"),
        name="mix",
    )(tok(x, D_MODEL), tok(oa, 256), tok(ga, 256), tok(ob, 256), tok(uc, 256), tok(gc, 256), tok(ud, 512),
      tok(gd, 256), pool0, conv0, w["pool_w"], w["pool_scale"], w["conv_w"], w["conv_b"], w["ln_g"], w["ln_b"],
      w["pw_out"], w["w_out"], final_g)


def _layer_weights(l, norm_g, w_in, fox_bf, rw_mu, rw_w0, rw_wup, rw_a0, rw_aup, rw_kk, rw_ka, rw_rk,
                   rw_gn_g, rw_gn_b, pool_w, pool_scale, conv_w, conv_b, ln_g, ln_b, pw_out, w_out):
    wi = w_in[l]
    o_f = 3 * D_BRANCH
    o_ga = o_f + N_HEADS
    main = jnp.concatenate([wi[:, 0:D_BRANCH], wi[:, o_ga:]], axis=1).astype(BF16)
    wkv_t = wi[:, D_BRANCH:o_f].T.astype(BF16)
    wf = jnp.zeros((8, D_MODEL), F32).at[0:N_HEADS].set(wi[:, o_f:o_ga].T).astype(BF16)
    bf = jnp.zeros((8, 1), F32).at[0:N_HEADS, 0].set(fox_bf[l])
    lora = jnp.zeros((2 * LORA, 2 * D_BRANCH), F32)
    lora = lora.at[0:LORA, 0:D_BRANCH].set(rw_wup[l]).at[LORA:, D_BRANCH:].set(rw_aup[l]).astype(BF16)
    pw_bd = jnp.zeros((D_BRANCH, D_BRANCH), F32)
    for g in range(len(POOL_WINDOWS)):
        gs = slice(g * HEAD_DIM, (g + 1) * HEAD_DIM)
        pw_bd = pw_bd.at[gs, gs].set(pool_w[l, g])

    def row(t):
        return t[l].reshape(1, -1)

    return dict(
        norm_g=row(norm_g), main=main, wkv_t=wkv_t, wf=wf, bf=bf,
        rwkv=dict(mu=row(rw_mu), w0=row(rw_w0), a0=row(rw_a0), lora=lora, kk=row(rw_kk), ka=row(rw_ka),
                  rk=row(rw_rk), gn_g=row(rw_gn_g), gn_b=row(rw_gn_b)),
        mix=dict(pool_w=pw_bd.astype(BF16), pool_scale=row(pool_scale), conv_w=conv_w[l], conv_b=row(conv_b),
                 ln_g=row(ln_g), ln_b=row(ln_b), pw_out=pw_out[l].astype(BF16), w_out=w_out[l].astype(BF16)))


def _stream_layer(x, lw, attend, wkv0, shift0, pool0, conv0, tiles, t_valid, pos0, flat_inproj, final_g, final):
    batch, seq, _ = x.shape
    x_in = x.reshape(1, batch * seq, D_MODEL) if flat_inproj else x
    *tok, kt, vt, lf_t = _inproj(x_in, lw["norm_g"], lw["main"], lw["wkv_t"], lw["wf"], lw["bf"], tiles["inproj"])
    q, ga, sh, gb, uc, gc, ud, gd = (t.reshape(batch, seq, t.shape[-1]) for t in tok)
    oa = attend(q, kt, vt, lf_t)
    if flat_inproj:
        prev0 = jnp.pad(shift0[:, None, :], ((0, 0), (0, seq - 1), (0, 0))).reshape(1, batch * seq, SHIFT_W)
        ob, wkv1 = _rwkv(sh.reshape(1, batch * seq, SHIFT_W), gb.reshape(1, batch * seq, D_BRANCH), wkv0, prev0,
                         lw["rwkv"], tiles["rwkv"], seq, t_valid, False)
        ob = ob.reshape(batch, seq, D_BRANCH)
    else:
        ob, wkv1 = _rwkv(sh, gb, wkv0, shift0[:, None, :], lw["rwkv"], tiles["rwkv"], tiles["chunk"],
                         tiles["chunk"], True)
    t_last = t_valid - (seq - tiles["mix"])
    x1, pool1, conv1 = _mix(x, oa, ga, ob, uc, gc, ud, gd, pool0, conv0, lw["mix"], final_g, batch, seq,
                            tiles["mix"], t_last, pos0, final)
    return x1, kt, vt, sh, lf_t, wkv1, pool1, conv1


def kernel(x_prompt, x_sample, cache_k, cache_v, cache_logf, state_wkv, state_shift, state_pool, state_conv,
           page_table, norm_g, w_in, fox_bf, rw_mu, rw_w0, rw_wup, rw_a0, rw_aup, rw_kk, rw_ka, rw_rk,
           rw_gn_g, rw_gn_b, pool_w, pool_scale, conv_w, conv_b, ln_g, ln_b, pw_out, w_out, final_norm_g):
    bp, tp, _ = x_prompt.shape
    bs, ts, _ = x_sample.shape
    depth = w_in.shape[0]
    n_phys = cache_k.shape[0]
    past = page_table.shape[1] * PAGE
    weights = (norm_g, w_in, fox_bf, rw_mu, rw_w0, rw_wup, rw_a0, rw_aup, rw_kk, rw_ka, rw_rk, rw_gn_g, rw_gn_b,
               pool_w, pool_scale, conv_w, conv_b, ln_g, ln_b, pw_out, w_out)

    tile_p = min(256, tp)
    tiles_p = dict(inproj=min(512, tp), rwkv=min(512, tp), chunk=min(64, tile_p), mix=tile_p)
    tiles_s = dict(inproj=min(256, bs * SAMPLE_PAD), rwkv=min(64, bs * SAMPLE_PAD), chunk=SAMPLE_PAD, mix=SAMPLE_PAD)

    xp = x_prompt
    xs = jnp.pad(x_sample, ((0, 0), (0, SAMPLE_PAD - ts), (0, 0)))
    cache_kt = jnp.transpose(cache_k, (0, 1, 3, 4, 2))
    cache_vt = jnp.transpose(cache_v, (0, 1, 3, 4, 2))
    cache_lft = jnp.transpose(cache_logf, (0, 1, 3, 2))

    wkv_zero = jnp.zeros((bp, N_HEADS, HEAD_DIM, HEAD_DIM), F32)
    shift_zero = jnp.zeros((bp, SHIFT_W), F32)
    pool_zero = jnp.zeros((bp, POOL_BUF, D_BRANCH), F32)
    conv_zero = jnp.zeros((bp, CONV_BUF, D_BRANCH), F32)

    def sample_heads(t):
        return t.reshape(N_HEADS, HEAD_DIM, bs, SAMPLE_PAD).transpose(2, 0, 1, 3)

    def pad_keys(t):
        return jnp.pad(t, [(0, 0)] * (t.ndim - 1) + [(0, PAGE - SAMPLE_PAD)])

    fg = final_norm_g.reshape(1, D_MODEL)
    outs_p, outs_s = [], []
    for l in range(depth):
        lw = _layer_weights(l, *weights)

        attend_p = lambda q, kt, vt, lf_t: _fox_prompt(q, kt, vt, lf_t, min(512, tp))
        xp, kt_p, vt_p, sh_p, lf_p, wkv_p, pool_p, conv_p = _stream_layer(
            xp, lw, attend_p, wkv_zero, shift_zero, pool_zero, conv_zero, tiles_p, tp, 0, False, fg, l == depth - 1)
        outs_p.append((kt_p.reshape(bp, N_HEADS, HEAD_DIM, tp), vt_p.reshape(bp, N_HEADS, HEAD_DIM, tp),
                       lf_p[:, 0:N_HEADS], wkv_p, sh_p[:, tp - 1], pool_p, conv_p))

        def attend_s(q, kt, vt, lf_t, l=l):
            lf_new = lf_t[0, 0:N_HEADS].reshape(N_HEADS, bs, SAMPLE_PAD).transpose(1, 0, 2)
            return _fox_sample(q, pad_keys(sample_heads(kt)), pad_keys(sample_heads(vt)), pad_keys(lf_new),
                               cache_kt, cache_vt, cache_lft, page_table, l, ts)

        xs, kt_s, vt_s, sh_s, lf_s, wkv_s, pool_s, conv_s = _stream_layer(
            xs, lw, attend_s, state_wkv[:, l], state_shift[:, l], state_pool[:, l], state_conv[:, l],
            tiles_s, ts, past, True, fg, l == depth - 1)
        outs_s.append((sample_heads(kt_s).transpose(0, 3, 1, 2)[:, 0:ts],
                       sample_heads(vt_s).transpose(0, 3, 1, 2)[:, 0:ts],
                       lf_s[0, 0:N_HEADS].reshape(N_HEADS, bs, SAMPLE_PAD).transpose(1, 2, 0)[:, 0:ts],
                       wkv_s, sh_s[:, ts - 1], pool_s, conv_s))

    y_prompt = xp
    y_sample = xs[:, 0:ts]

    def stk(outs, i):
        return jnp.stack([o[i] for o in outs], axis=1)

    k_prompt = stk(outs_p, 0).transpose(0, 1, 4, 2, 3)
    v_prompt = stk(outs_p, 1).transpose(0, 1, 4, 2, 3)
    logf_prompt = stk(outs_p, 2).transpose(0, 1, 3, 2)
    return (y_prompt, y_sample, k_prompt, v_prompt, logf_prompt,
            *(stk(outs_p, i) for i in range(3, 7)),
            *(stk(outs_s, i) for i in range(7)))
```

```python
import functools

import jax
import jax.numpy as jnp
from jax import lax
from jax.experimental import pallas as pl
from jax.experimental.pallas import tpu as pltpu

F32 = jnp.float32
BF16 = jnp.bfloat16

D_MODEL = 1024
D_BRANCH = 256
HEAD_DIM = 64
N_HEADS = 4
LORA = 64
SHIFT_W = 3 * D_BRANCH + 2 * LORA
PAGE = 128
POOL_WINDOWS = (2, 4, 8, 16)
POOL_BUF = 15
CONV_WIDTH = 31
CONV_BUF = 30
RMS_EPS = 1e-6
LN_EPS = 1e-5
GN_EPS = 64e-5
ATTN_SCALE = HEAD_DIM ** -0.5
NEG_BIG = -1e30
SAMPLE_PAD = 8
PAGES_PER_STEP = 32
RWKV_PIECE = 64
RWKV_GROUP = 32
VMEM_LIMIT = 56 * 1024 * 1024

NN = ((1,), (0,))
NT = ((1,), (1,))
TN = ((0,), (0,))


def _dg(a, b, dims=NN):
    return lax.dot_general(a, b, (dims, ((), ())), preferred_element_type=F32)


def _split3(x):
    hi = x.astype(BF16)
    r1 = x - hi.astype(F32)
    mid = r1.astype(BF16)
    lo = (r1 - mid.astype(F32)).astype(BF16)
    return hi, mid, lo


def _dot_f32(a, b, dims=NN):
    ah, am, _ = _split3(a)
    bh, bm, _ = _split3(b)
    return _dg(ah, bh, dims) + (_dg(ah, bm, dims) + _dg(am, bh, dims))


def _dot_f32_stacked(a, b):
    ah, am, _ = _split3(a)
    bh, bm, _ = _split3(b)
    return _dg(jnp.concatenate([ah, am, ah], axis=1), jnp.concatenate([bh, bh, bm], axis=0))


def _dot_exact_rhs(a, b_bf16, dims=NN):
    ah, am, al = _split3(a)
    return _dg(ah, b_bf16, dims) + (_dg(am, b_bf16, dims) + _dg(al, b_bf16, dims))


def _dot_exact_lhs(a_bf16, b, dims=NN):
    bh, bm, bl = _split3(b)
    return _dg(a_bf16, bh, dims) + (_dg(a_bf16, bm, dims) + _dg(a_bf16, bl, dims))


def _group_sum(a, ones_bf16):
    hi = a.astype(BF16)
    mid = (a - hi.astype(F32)).astype(BF16)
    return _dg(hi, ones_bf16) + _dg(mid, ones_bf16)


def _sigmoid(x):
    return 1.0 / (1.0 + jnp.exp(-x))


def _silu(x):
    return x * _sigmoid(x)


def _softplus(x):
    return jnp.maximum(x, 0.0) + jnp.log1p(jnp.exp(-jnp.abs(x)))


def _iota(shape, dim):
    return lax.broadcasted_iota(jnp.int32, shape, dim)


def _params(*sem):
    return pltpu.CompilerParams(dimension_semantics=sem, vmem_limit_bytes=VMEM_LIMIT)


_IN_SEGS = (("q", 256), ("ga", 256), ("sh", SHIFT_W), ("gb", 256),
            ("uc", 256), ("gc", 256), ("ud", 512), ("gd", 256))
_IN_COLS = sum(s for _, s in _IN_SEGS)


def _inproj_kernel(x_ref, g_ref, w_ref, wkv_ref, wf_ref, bf_ref, *out_refs):
    x = x_ref[0]
    ms = jnp.mean(x * x, axis=-1, keepdims=True)
    h = ((x * lax.rsqrt(ms + RMS_EPS)) * g_ref[...]).astype(BF16)
    off = 0
    for (_, size), o_ref in zip(_IN_SEGS, out_refs[:-3]):
        o_ref[0] = jnp.dot(h, w_ref[:, off:off + size], preferred_element_type=F32)
        off += size
    kt_ref, vt_ref, lf_ref = out_refs[-3:]
    kt_ref[0] = _dg(wkv_ref[0:D_BRANCH, :], h, NT)
    vt_ref[0] = _dg(wkv_ref[D_BRANCH:2 * D_BRANCH, :], h, NT)
    f = _dg(wf_ref[...], h, NT) + bf_ref[...]
    lf_ref[0] = -_softplus(-f)


def _inproj(x, g, w, wkv_t, wf, bf, tm):
    batch, seq, _ = x.shape
    out_shape = [jax.ShapeDtypeStruct((batch, seq, s), F32) for _, s in _IN_SEGS]
    out_shape += [jax.ShapeDtypeStruct((batch, D_BRANCH, seq), F32)] * 2
    out_shape.append(jax.ShapeDtypeStruct((batch, 8, seq), F32))
    out_specs = [pl.BlockSpec((1, tm, s), lambda b, i: (b, i, 0)) for _, s in _IN_SEGS]
    out_specs += [pl.BlockSpec((1, D_BRANCH, tm), lambda b, i: (b, 0, i))] * 2
    out_specs.append(pl.BlockSpec((1, 8, tm), lambda b, i: (b, 0, i)))

    def full(shape):
        return pl.BlockSpec(shape, lambda b, i: (0,) * len(shape))

    return pl.pallas_call(
        _inproj_kernel,
        grid=(batch, seq // tm),
        in_specs=[pl.BlockSpec((1, tm, D_MODEL), lambda b, i: (b, i, 0)),
                  full((1, D_MODEL)), full((D_MODEL, _IN_COLS)), full((2 * D_BRANCH, D_MODEL)),
                  full((8, D_MODEL)), full((8, 1))],
        out_specs=out_specs,
        out_shape=out_shape,
        compiler_params=_params("parallel", "parallel"),
        name="inproj",
    )(x, g, w, wkv_t, wf, bf)


def _fox_prompt_kernel(q_ref, kt_ref, vt_ref, lf_ref, o_ref, c_ref, kb_ref, vb_ref, m_ref, acc_ref, *, tq, seq):
    i = pl.program_id(1)
    heads = [slice(h * HEAD_DIM, (h + 1) * HEAD_DIM) for h in range(N_HEADS)]

    @pl.when(i == 0)
    def _per_sequence():
        upper = (_iota((128, 128), 0) <= _iota((128, 128), 1)).astype(BF16)
        carry = jnp.zeros((8, 1), F32)
        for blk in range(seq // 128):
            cs = _dot_exact_rhs(lf_ref[0, :, blk * 128:(blk + 1) * 128], upper) + carry
            c_ref[:, blk * 128:(blk + 1) * 128] = cs
            carry = cs[:, 127:128]
        kb_ref[...] = kt_ref[0].astype(BF16)
        for h, hs in enumerate(heads):
            vb_ref[h, 0:HEAD_DIM, :] = vt_ref[0, hs, :].astype(BF16)
            vb_ref[h, HEAD_DIM:2 * HEAD_DIM, :] = jnp.ones((HEAD_DIM, seq), BF16)

    row0 = pl.multiple_of(i * tq, tq)
    c_tile = c_ref[:, pl.ds(row0, tq)]
    causal = _iota((tq, tq), 1) <= _iota((tq, tq), 0)

    q_all = q_ref[0] * ATTN_SCALE
    q_heads = [q_all[:, hs].astype(BF16) for hs in heads]
    m_ref[...] = jnp.full(m_ref.shape, NEG_BIG, F32)
    acc_ref[...] = jnp.zeros(acc_ref.shape, F32)

    def tile_update(col0, masked):
        s_all = []
        for h, hs in enumerate(heads):
            kh = kb_ref[hs, pl.ds(col0, tq)]
            s = _dg(q_heads[h], kh) + (c_tile[h:h + 1, 0:1] - c_ref[h:h + 1, pl.ds(col0, tq)])
            s_all.append(jnp.where(causal, s, NEG_BIG) if masked else s)
        m_old = [m_ref[h] for h in range(N_HEADS)]
        m_new = [jnp.maximum(mo, jnp.max(s, axis=-1, keepdims=True)) for mo, s in zip(m_old, s_all)]
        p_all = [jnp.exp(s - jnp.tile(mn, (1, tq // 128))).astype(BF16) for s, mn in zip(s_all, m_new)]
        pv = [_dg(p, vb_ref[h, :, pl.ds(col0, tq)], NT) for h, p in enumerate(p_all)]
        for h in range(N_HEADS):
            acc_ref[h] = jnp.exp(m_old[h] - m_new[h]) * acc_ref[h] + pv[h]
            m_ref[h] = m_new[h]

    def body(j, carry):
        tile_update(pl.multiple_of(j * tq, tq), False)
        return carry

    lax.fori_loop(0, i, body, 0)
    tile_update(row0, True)
    for h, hs in enumerate(heads):
        a = acc_ref[h]
        o_ref[0, :, hs] = a[:, 0:HEAD_DIM] / a[:, HEAD_DIM:2 * HEAD_DIM]


def _fox_prompt(q, kt, vt, lf_t, tq):
    batch, seq, _ = q.shape
    kern = functools.partial(_fox_prompt_kernel, tq=tq, seq=seq)
    return pl.pallas_call(
        kern,
        grid=(batch, seq // tq),
        in_specs=[pl.BlockSpec((1, tq, D_BRANCH), lambda b, i: (b, i, 0)),
                  pl.BlockSpec((1, D_BRANCH, seq), lambda b, i: (b, 0, 0)),
                  pl.BlockSpec((1, D_BRANCH, seq), lambda b, i: (b, 0, 0)),
                  pl.BlockSpec((1, 8, seq), lambda b, i: (b, 0, 0))],
        out_specs=pl.BlockSpec((1, tq, D_BRANCH), lambda b, i: (b, i, 0)),
        out_shape=jax.ShapeDtypeStruct((batch, seq, D_BRANCH), F32),
        scratch_shapes=[pltpu.VMEM((8, seq), F32),
                        pltpu.VMEM((D_BRANCH, seq), BF16), pltpu.VMEM((N_HEADS, 2 * HEAD_DIM, seq), BF16),
                        pltpu.VMEM((N_HEADS, tq, 128), F32), pltpu.VMEM((N_HEADS, tq, 2 * HEAD_DIM), F32)],
        compiler_params=_params("parallel", "arbitrary"),
        name="fox_prompt",
    )(q, kt, vt, lf_t)


def _fox_sample_kernel(pt_ref, q_ref, kn_ref, vn_ref, lfn_ref, ck_ref, cv_ref, clf_ref, o_ref,
                       kbuf, vbuf, lfbuf, suffix_ref, total_ref, s_ref, ksem, vsem, lfsem,
                       *, layer, n_pages, t_valid):
    b = pl.program_id(0)
    slot = b % 2
    rows = N_HEADS * SAMPLE_PAD
    group = min(PAGES_PER_STEP, n_pages)

    def page_copies(seq, slot_, p):
        page = pt_ref[seq, p]
        return (pltpu.make_async_copy(ck_ref.at[page, layer], kbuf.at[slot_, p], ksem.at[slot_]),
                pltpu.make_async_copy(cv_ref.at[page, layer], vbuf.at[slot_, p], vsem.at[slot_]),
                pltpu.make_async_copy(clf_ref.at[page, layer], lfbuf.at[slot_, p, 0:N_HEADS], lfsem.at[slot_]))

    def start_fetch(seq, slot_):
        def body(p, carry):
            for cp in page_copies(seq, slot_, p):
                cp.start()
            return carry
        lax.fori_loop(0, n_pages, body, 0)

    def wait_fetch(seq, slot_):
        def body(p, carry):
            for cp in page_copies(seq, slot_, p):
                cp.wait()
            return carry
        lax.fori_loop(0, n_pages, body, 0)

    @pl.when(b == 0)
    def _first():
        lfbuf[...] = jnp.zeros(lfbuf.shape, F32)
        start_fetch(0, 0)

    @pl.when(b + 1 < pl.num_programs(0))
    def _prefetch_next():
        start_fetch(b + 1, 1 - slot)

    wait_fetch(b, slot)

    q_all = q_ref[0] * ATTN_SCALE
    q_heads = [q_all[:, h * HEAD_DIM:(h + 1) * HEAD_DIM].astype(BF16) for h in range(N_HEADS)]

    key4 = _iota((N_HEADS, PAGE), 1)
    lfbuf[slot, n_pages, 0:N_HEADS, :] = jnp.where(key4 < t_valid, lfn_ref[0], 0.0)
    lf_all = lfbuf[slot].reshape((n_pages + 1) * 8, PAGE)
    later = (_iota((PAGE, PAGE), 0) > _iota((PAGE, PAGE), 1)).astype(BF16)
    suffix_ref[...] = _dot_exact_rhs(lf_all, later)
    total_ref[...] = _dot_exact_rhs(lf_all, jnp.ones((PAGE, PAGE), BF16))

    def scores(kt, bias8, col0, mask):
        for h in range(N_HEADS):
            s = _dg(q_heads[h], kt[h].astype(BF16)) + bias8[h:h + 1, :]
            if mask is not None:
                s = jnp.where(mask, s, NEG_BIG)
            s_ref[h * SAMPLE_PAD:(h + 1) * SAMPLE_PAD, pl.ds(col0, PAGE)] = s

    new0 = n_pages * 8
    key = _iota((SAMPLE_PAD, PAGE), 1)
    qry = _iota((SAMPLE_PAD, PAGE), 0)
    scores(kn_ref[0], suffix_ref[new0:new0 + 8, :], n_pages * PAGE,
           jnp.logical_and(key <= qry, key < t_valid))

    def score_group(g, running):
        lo = n_pages - (g + 1) * group
        bias = [None] * group
        for i in reversed(range(group)):
            r0 = pl.multiple_of((lo + i) * 8, 8)
            bias[i] = suffix_ref[pl.ds(r0, 8), :] + running
            running = running + total_ref[pl.ds(r0, 8), :]
        col0 = pl.multiple_of(lo * PAGE, group * PAGE)
        for h in range(N_HEADS):
            kt_wide = jnp.concatenate([kbuf[slot, lo + i, h].astype(BF16) for i in range(group)], axis=1)
            bias_h = jnp.concatenate([b8[h:h + 1, :] for b8 in bias], axis=1)
            s_ref[h * SAMPLE_PAD:(h + 1) * SAMPLE_PAD, pl.ds(col0, group * PAGE)] = _dg(q_heads[h], kt_wide) + bias_h
        return running

    lax.fori_loop(0, n_pages // group, score_group, total_ref[new0:new0 + 8, :])

    s = s_ref[...]
    m = jnp.max(s, axis=-1, keepdims=True)
    p_all = jnp.exp(s - m)
    denom = jnp.sum(p_all, axis=-1, keepdims=True)
    s_ref[...] = p_all

    def weighted(vt, col0):
        return jnp.concatenate(
            [_dg(s_ref[h * SAMPLE_PAD:(h + 1) * SAMPLE_PAD, pl.ds(col0, PAGE)].astype(BF16),
                 vt[h].astype(BF16), NT) for h in range(N_HEADS)], axis=0)

    def value_group(g, acc):
        lo = g * group
        col0 = pl.multiple_of(lo * PAGE, group * PAGE)
        parts = []
        for h in range(N_HEADS):
            vt_wide = jnp.concatenate([vbuf[slot, lo + i, h].astype(BF16) for i in range(group)], axis=1)
            p_wide = s_ref[h * SAMPLE_PAD:(h + 1) * SAMPLE_PAD, pl.ds(col0, group * PAGE)].astype(BF16)
            parts.append(_dg(p_wide, vt_wide, NT))
        return acc + jnp.concatenate(parts, axis=0)

    acc = lax.fori_loop(0, n_pages // group, value_group, weighted(vn_ref[0], n_pages * PAGE))
    out = acc / denom
    for h in range(N_HEADS):
        o_ref[0, :, h * HEAD_DIM:(h + 1) * HEAD_DIM] = out[h * SAMPLE_PAD:(h + 1) * SAMPLE_PAD]


def _fox_sample(q, kt_new, vt_new, lf_new, cache_kt, cache_vt, cache_lft, page_table, layer, t_valid):
    batch, n_pages = page_table.shape
    rows = N_HEADS * SAMPLE_PAD
    assert n_pages % min(PAGES_PER_STEP, n_pages) == 0

    def per_seq(shape):
        return pl.BlockSpec((1,) + shape, lambda b, pt: (b,) + (0,) * len(shape))

    in_specs = [per_seq((SAMPLE_PAD, D_BRANCH)), per_seq((N_HEADS, HEAD_DIM, PAGE)),
                per_seq((N_HEADS, HEAD_DIM, PAGE)), per_seq((N_HEADS, PAGE)),
                pl.BlockSpec(memory_space=pl.ANY), pl.BlockSpec(memory_space=pl.ANY),
                pl.BlockSpec(memory_space=pl.ANY)]
    kern = functools.partial(_fox_sample_kernel, layer=layer, n_pages=n_pages, t_valid=t_valid)
    return pl.pallas_call(
        kern,
        grid_spec=pltpu.PrefetchScalarGridSpec(
            num_scalar_prefetch=1,
            grid=(batch,),
            in_specs=in_specs,
            out_specs=per_seq((SAMPLE_PAD, D_BRANCH)),
            scratch_shapes=[pltpu.VMEM((2, n_pages, N_HEADS, HEAD_DIM, PAGE), F32),
                            pltpu.VMEM((2, n_pages, N_HEADS, HEAD_DIM, PAGE), F32),
                            pltpu.VMEM((2, n_pages + 1, 8, PAGE), F32),
                            pltpu.VMEM(((n_pages + 1) * 8, PAGE), F32),
                            pltpu.VMEM(((n_pages + 1) * 8, PAGE), F32),
                            pltpu.VMEM((rows, (n_pages + 1) * PAGE), F32),
                            pltpu.SemaphoreType.DMA((2,)), pltpu.SemaphoreType.DMA((2,)),
                            pltpu.SemaphoreType.DMA((2,))]),
        out_shape=jax.ShapeDtypeStruct((batch, SAMPLE_PAD, D_BRANCH), F32),
        compiler_params=_params("arbitrary"),
        name="fox_sample",
    )(page_table, q, kt_new, vt_new, lf_new, cache_kt, cache_vt, cache_lft)


def _unit_lower_inverses(ns, size, chunk):
    r = _iota((size, size), 0)
    c = _iota((size, size), 1)
    eye = (r == c).astype(F32)
    pair = (r // 2) == (c // 2)
    ms = [eye - jnp.where(pair, n.astype(F32), 0.0) for n in ns]
    blk = 2
    while blk < chunk:
        off = jnp.logical_and((r // (2 * blk)) == (c // (2 * blk)), (r // blk) != (c // blk))
        mbs = [m.astype(BF16) for m in ms]
        ts = [_dg(mb, n).astype(BF16) for mb, n in zip(mbs, ns)]
        ts = [_dg(t, mb) for t, mb in zip(ts, mbs)]
        ms = [m - jnp.where(off, t, 0.0) for m, t in zip(ms, ts)]
        blk *= 2
    return ms


def _rwkv_kernel(sh_ref, gb_ref, s0_ref, prev0_ref, mu_ref, w0_ref, a0_ref, lora_ref, kkw_ref, ka_ref,
                 rk_ref, gng_ref, gnb_ref, o_ref, s1_ref, ext_ref, st_ref, oacc_ref, *, tb, chunk, t_valid,
                 carried):
    n_blk = tb // chunk
    sh = sh_ref[0]
    if carried:
        i = pl.program_id(1)

        @pl.when(i == 0)
        def _init():
            st_ref[...] = s0_ref[0]
            ext_ref[0:8, :] = jnp.zeros((8, SHIFT_W), F32)
            ext_ref[7:8, :] = prev0_ref[0]

        @pl.when(i > 0)
        def _carry_shift():
            ext_ref[7:8, :] = ext_ref[tb + 7:tb + 8, :]

        ext_ref[8:8 + tb, :] = sh
        prev = ext_ref[7:7 + tb, :]
    else:
        ext_ref[0:8, :] = jnp.zeros((8, SHIFT_W), F32)
        ext_ref[8:8 + tb, :] = sh
        first = (_iota((tb, 1), 0) % chunk) == 0
        prev = jnp.where(first, prev0_ref[0], ext_ref[7:7 + tb, :])
    xm = sh + mu_ref[...] * (prev - sh)
    r = xm[:, 0:256]
    k = xm[:, 256:512]
    v = xm[:, 512:768]
    wa = xm[:, 768:896]
    wa = jnp.where(_iota((tb, 2 * LORA), 1) < LORA, jnp.tanh(wa), wa)
    lora = jnp.dot(wa.astype(BF16), lora_ref[...], preferred_element_type=F32)
    w_log = -_softplus(-(w0_ref[...] + lora[:, 0:256])) - 0.5
    log_decay = -jnp.exp(w_log)
    a = _sigmoid(a0_ref[...] + lora[:, 256:512])

    grp = ((_iota((D_BRANCH, D_BRANCH), 0) // HEAD_DIM) == (_iota((D_BRANCH, D_BRANCH), 1) // HEAD_DIM)).astype(BF16)
    kk = k * kkw_ref[...]
    kk = kk * lax.rsqrt(jnp.maximum(_group_sum(kk * kk, grp), 1e-12))
    kp = k * (1.0 + (a - 1.0) * ka_ref[...])
    beta = kk * a
    bonus = _group_sum(r * kp * rk_ref[...], grp) * v
    if t_valid < chunk:
        keep = ((_iota((tb, 1), 0) % chunk) < t_valid).astype(F32)
        log_decay = log_decay * keep
        kp = kp * keep
        beta = beta * keep

    rr = _iota((tb, tb), 0)
    cc = _iota((tb, tb), 1)
    incl = jnp.logical_and((rr // chunk) == (cc // chunk), rr >= cc)

    lc = _dot_exact_lhs(incl.astype(BF16), log_decay)
    lc_tot = jnp.concatenate(
        [jnp.broadcast_to(lc[(c + 1) * chunk - 1:(c + 1) * chunk, :], (chunk, D_BRANCH)) for c in range(n_blk)],
        axis=0)
    to_end = jnp.exp(lc_tot - lc)
    g_inv = jnp.exp(-lc)
    g_end = jnp.exp(lc_tot)
    qc = kk * jnp.exp(lc - log_decay)
    rc = r * jnp.exp(lc)
    kc = kp * g_inv
    bc = beta * g_inv
    kd = kp * to_end
    bd = beta * to_end
    eye_k = (_iota((HEAD_DIM, HEAD_DIM), 0) == _iota((HEAD_DIM, HEAD_DIM), 1)).astype(F32)

    def b16(t):
        return t.astype(BF16)

    mat = min(tb, RWKV_PIECE)
    n_sub = tb // mat
    rg = _iota((2 * mat, 2 * mat), 0)
    rl = rg % mat
    cl = _iota((2 * mat, 2 * mat), 1) % mat
    g_mask = jnp.logical_and((rl // chunk) == (cl // chunk), rl + rg // mat > cl)
    pieces = [(sub, h) for sub in range(n_sub) for h in range(N_HEADS)]

    def piece(t, sub, h):
        return t[sub * mat:(sub + 1) * mat, h * HEAD_DIM:(h + 1) * HEAD_DIM]

    zero_b = jnp.zeros((mat, HEAD_DIM), BF16)
    r_f, v_f, v_b, kd_f, bd_f, w_f, y_f, o0, rp = ([] for _ in range(9))
    for g0 in range(0, len(pieces), RWKV_GROUP):
        grp_pieces = pieces[g0:g0 + RWKV_GROUP]
        r_g = [piece(rc, *ph) for ph in grp_pieces]
        v_g = [piece(v, *ph) for ph in grp_pieces]
        q_b = [b16(piece(qc, *ph)) for ph in grp_pieces]
        k_b = [b16(piece(kc, *ph)) for ph in grp_pieces]
        b_b = [b16(piece(bc, *ph)) for ph in grp_pieces]
        r_b = [b16(t) for t in r_g]
        vb_g = [b16(t) for t in v_g]
        g_all = [b16(jnp.where(g_mask, _dg(jnp.concatenate([q, rr_], axis=0), jnp.concatenate([bb, k], axis=0), NT),
                               0.0)) for q, rr_, bb, k in zip(q_b, r_b, b_b, k_b)]
        g_top = [g[0:mat] for g in g_all]
        g_bot = [g[mat:2 * mat] for g in g_all]
        inv = [b16(t) for t in _unit_lower_inverses([g[:, 0:mat] for g in g_top], mat, chunk)]
        w_g = [_dg(m_, q) for m_, q in zip(inv, q_b)]
        av_b = [b16(_dg(g, jnp.concatenate([zero_b, vv], axis=0))) for g, vv in zip(g_top, vb_g)]
        y_g = [_dg(m_, av) for m_, av in zip(inv, av_b)]
        o0 += [_dg(g, jnp.concatenate([b16(-yy), vv], axis=0)) for g, yy, vv in zip(g_bot, y_g, vb_g)]
        rp += [rr_ - _dg(g, jnp.concatenate([b16(ww), zero_b], axis=0)) for rr_, g, ww in zip(r_g, g_bot, w_g)]
        r_f += r_g
        v_f += v_g
        v_b += vb_g
        w_f += w_g
        y_f += y_g
        kd_f += [piece(kd, *ph) for ph in grp_pieces]
        bd_f += [piece(bd, *ph) for ph in grp_pieces]

    blocks_per_sub = mat // chunk
    def block_piece(c, h):
        sub, loc = divmod(c, blocks_per_sub)
        return sub * N_HEADS + h, slice(loc * chunk, (loc + 1) * chunk)

    p_mats, z_mats = {}, {}
    for c in range(n_blk):
        for h in range(N_HEADS):
            idx, ls = block_piece(c, h)
            bd_c = b16(bd_f[idx][ls])
            g_c = g_end[c * chunk:c * chunk + 1, h * HEAD_DIM:(h + 1) * HEAD_DIM]
            p_mats[c, h] = eye_k * g_c - _dg(b16(w_f[idx][ls]), bd_c, TN)
            z_mats[c, h] = _dg(v_b[idx][ls] if chunk % 16 == 0 else b16(v_f[idx][ls]), b16(kd_f[idx][ls]), TN) \
                - _dg(b16(y_f[idx][ls]), bd_c, TN)

    states = [st_ref[h] for h in range(N_HEADS)] if carried else None
    s_start = {}
    for c in range(n_blk):
        for h in range(N_HEADS):
            s_in = states[h] if carried else s0_ref[c, h]
            s_start[c, h] = s_in
            s_out = _dot_f32(s_in, p_mats[c, h]) + z_mats[c, h]
            if carried:
                states[h] = s_out
            else:
                s1_ref[c, h] = s_out
    if carried:
        for h in range(N_HEADS):
            st_ref[h] = states[h]

    for c in range(n_blk):
        for h in range(N_HEADS):
            idx, ls = block_piece(c, h)
            oacc_ref[c * chunk:(c + 1) * chunk, h * HEAD_DIM:(h + 1) * HEAD_DIM] = \
                o0[idx][ls] + _dg(b16(rp[idx][ls]), b16(s_start[c, h]), NT)

    o = oacc_ref[...]
    mean = _group_sum(o, grp) * (1.0 / HEAD_DIM)
    d = o - mean
    var = _group_sum(d * d, grp) * (1.0 / HEAD_DIM)
    o = d * lax.rsqrt(var + GN_EPS) * gng_ref[...] + gnb_ref[...] + bonus
    o_ref[0] = o * _silu(gb_ref[0])
    if carried:
        s1_ref[0] = st_ref[...]


def _rwkv(sh, gb, s0, prev0, w, tb, chunk, t_valid, carried):
    batch, seq, _ = sh.shape
    kern = functools.partial(_rwkv_kernel, tb=tb, chunk=chunk, t_valid=t_valid, carried=carried)

    def row(width):
        return pl.BlockSpec((1, width), lambda b, i: (0, 0))

    if carried:
        state_spec = pl.BlockSpec((1, N_HEADS, HEAD_DIM, HEAD_DIM), lambda b, i: (b, 0, 0, 0))
        prev_spec = pl.BlockSpec((1, 1, SHIFT_W), lambda b, i: (b, 0, 0))
    else:
        state_spec = pl.BlockSpec((tb // chunk, N_HEADS, HEAD_DIM, HEAD_DIM), lambda b, i: (i, 0, 0, 0))
        prev_spec = pl.BlockSpec((1, tb, SHIFT_W), lambda b, i: (b, i, 0))
    return pl.pallas_call(
        kern,
        grid=(batch, seq // tb),
        in_specs=[pl.BlockSpec((1, tb, SHIFT_W), lambda b, i: (b, i, 0)),
                  pl.BlockSpec((1, tb, D_BRANCH), lambda b, i: (b, i, 0)),
                  state_spec, prev_spec,
                  row(SHIFT_W), row(D_BRANCH), row(D_BRANCH),
                  pl.BlockSpec((2 * LORA, 2 * D_BRANCH), lambda b, i: (0, 0)),
                  row(D_BRANCH), row(D_BRANCH), row(D_BRANCH), row(D_BRANCH), row(D_BRANCH)],
        out_specs=[pl.BlockSpec((1, tb, D_BRANCH), lambda b, i: (b, i, 0)), state_spec],
        out_shape=[jax.ShapeDtypeStruct((batch, seq, D_BRANCH), F32),
                   jax.ShapeDtypeStruct(s0.shape, F32)],
        scratch_shapes=[pltpu.VMEM((tb + 8, SHIFT_W), F32),
                        pltpu.VMEM((N_HEADS, HEAD_DIM, HEAD_DIM), F32),
                        pltpu.VMEM((tb, D_BRANCH), F32)],
        compiler_params=_params("parallel", "arbitrary"),
        name="rwkv",
    )(sh, gb, s0, prev0, w["mu"], w["w0"], w["a0"], w["lora"], w["kk"], w["ka"],
      w["rk"], w["gn_g"], w["gn_b"])


def _mix_kernel(x_ref, oa_ref, ga_ref, ob_ref, uc_ref, gc_ref, ud_ref, gd_ref, pool0_ref, conv0_ref,
                poolw_ref, pscale_ref, convw_ref, convb_ref, lng_ref, lnb_ref, pw_ref, wout_ref, fg_ref,
                xo_ref, pool1_ref, conv1_ref, pext_ref, cext_ref, pshift_ref, cshift_ref,
                *, tm, t_last, pos0, final):
    i = pl.program_id(1)
    pad_p = POOL_BUF + 1
    pad_c = CONV_BUF + 2

    @pl.when(i == 0)
    def _init():
        pext_ref[0:pad_p, :] = jnp.zeros((pad_p, D_BRANCH), F32)
        cext_ref[0:pad_c, :] = jnp.zeros((pad_c, D_BRANCH), F32)
        pext_ref[1:pad_p, :] = pool0_ref[0]
        cext_ref[2:pad_c, :] = conv0_ref[0]

    @pl.when(i > 0)
    def _carry():
        pext_ref[1:pad_p, :] = pext_ref[tm + 1:tm + pad_p, :]
        cext_ref[2:pad_c, :] = cext_ref[tm + 2:tm + pad_c, :]

    uc = uc_ref[0]
    pext_ref[pad_p:pad_p + tm, :] = uc
    ud = ud_ref[0]
    cext_ref[pad_c:pad_c + tm, :] = ud[:, 0:D_BRANCH] * _sigmoid(ud[:, D_BRANCH:2 * D_BRANCH])

    done_p = set()

    def back(d):
        a, sub = divmod(pad_p - d, 8)
        if sub not in done_p:
            done_p.add(sub)
            pshift_ref[sub] = pext_ref[sub:sub + tm + 8, :]
        return pshift_ref[sub, 8 * a:8 * a + tm, :]

    win2 = uc + back(1)
    win4 = win2 + back(2) + back(3)
    win8 = win4
    for d in range(4, 8):
        win8 = win8 + back(d)
    win16 = win8
    for d in range(8, 16):
        win16 = win16 + back(d)
    group = _iota((tm, D_BRANCH), 1) // HEAD_DIM
    win = jnp.where(group == 0, win2, jnp.where(group == 1, win4, jnp.where(group == 2, win8, win16)))
    width = jnp.where(group == 0, 2, jnp.where(group == 1, 4, jnp.where(group == 2, 8, 16)))
    pos = pos0 + i * tm + _iota((tm, D_BRANCH), 0)
    cnt = jnp.minimum(pos + 1, width).astype(F32)
    pooled = win / cnt - uc
    o_c = jnp.dot(pooled.astype(BF16), poolw_ref[...], preferred_element_type=F32) * pscale_ref[...]

    y = jnp.zeros((tm, D_BRANCH), F32) + convb_ref[...]
    for sub in range(8):
        taps = range(sub, CONV_WIDTH, 8)
        span = tm + 8 * (len(taps) - 1)
        cshift_ref[sub, 0:span, :] = cext_ref[2 + sub:2 + sub + span, :]
        for a, jt in enumerate(taps):
            y = y + cshift_ref[sub, 8 * a:8 * a + tm, :] * convw_ref[jt:jt + 1, :]
    mean = jnp.mean(y, axis=-1, keepdims=True)
    yc = y - mean
    var = jnp.mean(yc * yc, axis=-1, keepdims=True)
    yn = yc * lax.rsqrt(var + LN_EPS) * lng_ref[...] + lnb_ref[...]
    o_d = jnp.dot(_silu(yn).astype(BF16), pw_ref[...], preferred_element_type=F32)

    acc = jnp.dot((oa_ref[0] * _silu(ga_ref[0])).astype(BF16), wout_ref[0:256, :], preferred_element_type=F32)
    acc = acc + jnp.dot(ob_ref[0].astype(BF16), wout_ref[256:512, :], preferred_element_type=F32)
    acc = acc + jnp.dot((o_c * _silu(gc_ref[0])).astype(BF16), wout_ref[512:768, :], preferred_element_type=F32)
    acc = acc + jnp.dot((o_d * _silu(gd_ref[0])).astype(BF16), wout_ref[768:1024, :], preferred_element_type=F32)
    x_new = x_ref[0] + acc
    if final:
        ms = jnp.mean(x_new * x_new, axis=-1, keepdims=True)
        x_new = (x_new * lax.rsqrt(ms + RMS_EPS)) * fg_ref[...]
    xo_ref[0] = x_new

    pool1_ref[0] = pext_ref[t_last + 1:t_last + pad_p, :]
    conv1_ref[0] = cext_ref[t_last + 2:t_last + pad_c, :]


def _mix(x, oa, ga, ob, uc, gc, ud, gd, pool0, conv0, w, final_g, batch, seq, tm, t_last, pos0, final):
    def tok(t, width):
        return t.reshape(batch, seq, width)

    def tok_spec(width):
        return pl.BlockSpec((1, tm, width), lambda b, i: (b, i, 0))

    def full(shape):
        return pl.BlockSpec(shape, lambda b, i: (0,) * len(shape))

    pool_spec = pl.BlockSpec((1, POOL_BUF, D_BRANCH), lambda b, i: (b, 0, 0))
    conv_spec = pl.BlockSpec((1, CONV_BUF, D_BRANCH), lambda b, i: (b, 0, 0))
    kern = functools.partial(_mix_kernel, tm=tm, t_last=t_last, pos0=pos0, final=final)
    return pl.pallas_call(
        kern,
        grid=(batch, seq // tm),
        in_specs=[tok_spec(D_MODEL), tok_spec(256), tok_spec(256), tok_spec(256), tok_spec(256), tok_spec(256),
                  tok_spec(512), tok_spec(256), pool_spec, conv_spec,
                  full((D_BRANCH, D_BRANCH)), full((1, D_BRANCH)), full((CONV_WIDTH, D_BRANCH)),
                  full((1, D_BRANCH)), full((1, D_BRANCH)), full((1, D_BRANCH)),
                  full((D_BRANCH, D_BRANCH)), full((D_MODEL, D_MODEL)), full((1, D_MODEL))],
        out_specs=[tok_spec(D_MODEL), pool_spec, conv_spec],
        out_shape=[jax.ShapeDtypeStruct((batch, seq, D_MODEL), F32),
                   jax.ShapeDtypeStruct((batch, POOL_BUF, D_BRANCH), F32),
                   jax.ShapeDtypeStruct((batch, CONV_BUF, D_BRANCH), F32)],
        scratch_shapes=[pltpu.VMEM((POOL_BUF + 1 + tm, D_BRANCH), F32),
                        pltpu.VMEM((CONV_BUF + 2 + tm, D_BRANCH), F32),
                        pltpu.VMEM((8, tm + 8, D_BRANCH), F32),
                        pltpu.VMEM((8, tm + 8 * ((CONV_WIDTH - 1) // 8), D_BRANCH), F32)],
        compiler_params=_params("parallel", "arbitrary"),
        name="mix",
    )(tok(x, D_MODEL), tok(oa, 256), tok(ga, 256), tok(ob, 256), tok(uc, 256), tok(gc, 256), tok(ud, 512),
      tok(gd, 256), pool0, conv0, w["pool_w"], w["pool_scale"], w["conv_w"], w["conv_b"], w["ln_g"], w["ln_b"],
      w["pw_out"], w["w_out"], final_g)


def _layer_weights(l, norm_g, w_in, fox_bf, rw_mu, rw_w0, rw_wup, rw_a0, rw_aup, rw_kk, rw_ka, rw_rk,
                   rw_gn_g, rw_gn_b, pool_w, pool_scale, conv_w, conv_b, ln_g, ln_b, pw_out, w_out):
    wi = w_in[l]
    o_f = 3 * D_BRANCH
    o_ga = o_f + N_HEADS
    main = jnp.concatenate([wi[:, 0:D_BRANCH], wi[:, o_ga:]], axis=1).astype(BF16)
    wkv_t = wi[:, D_BRANCH:o_f].T.astype(BF16)
    wf = jnp.zeros((8, D_MODEL), F32).at[0:N_HEADS].set(wi[:, o_f:o_ga].T).astype(BF16)
    bf = jnp.zeros((8, 1), F32).at[0:N_HEADS, 0].set(fox_bf[l])
    lora = jnp.zeros((2 * LORA, 2 * D_BRANCH), F32)
    lora = lora.at[0:LORA, 0:D_BRANCH].set(rw_wup[l]).at[LORA:, D_BRANCH:].set(rw_aup[l]).astype(BF16)
    pw_bd = jnp.zeros((D_BRANCH, D_BRANCH), F32)
    for g in range(len(POOL_WINDOWS)):
        gs = slice(g * HEAD_DIM, (g + 1) * HEAD_DIM)
        pw_bd = pw_bd.at[gs, gs].set(pool_w[l, g])

    def row(t):
        return t[l].reshape(1, -1)

    return dict(
        norm_g=row(norm_g), main=main, wkv_t=wkv_t, wf=wf, bf=bf,
        rwkv=dict(mu=row(rw_mu), w0=row(rw_w0), a0=row(rw_a0), lora=lora, kk=row(rw_kk), ka=row(rw_ka),
                  rk=row(rw_rk), gn_g=row(rw_gn_g), gn_b=row(rw_gn_b)),
        mix=dict(pool_w=pw_bd.astype(BF16), pool_scale=row(pool_scale), conv_w=conv_w[l], conv_b=row(conv_b),
                 ln_g=row(ln_g), ln_b=row(ln_b), pw_out=pw_out[l].astype(BF16), w_out=w_out[l].astype(BF16)))


def _stream_layer(x, lw, attend, wkv0, shift0, pool0, conv0, tiles, t_valid, pos0, flat_inproj, final_g, final):
    batch, seq, _ = x.shape
    x_in = x.reshape(1, batch * seq, D_MODEL) if flat_inproj else x
    *tok, kt, vt, lf_t = _inproj(x_in, lw["norm_g"], lw["main"], lw["wkv_t"], lw["wf"], lw["bf"], tiles["inproj"])
    q, ga, sh, gb, uc, gc, ud, gd = (t.reshape(batch, seq, t.shape[-1]) for t in tok)
    oa = attend(q, kt, vt, lf_t)
    if flat_inproj:
        prev0 = jnp.pad(shift0[:, None, :], ((0, 0), (0, seq - 1), (0, 0))).reshape(1, batch * seq, SHIFT_W)
        ob, wkv1 = _rwkv(sh.reshape(1, batch * seq, SHIFT_W), gb.reshape(1, batch * seq, D_BRANCH), wkv0, prev0,
                         lw["rwkv"], tiles["rwkv"], seq, t_valid, False)
        ob = ob.reshape(batch, seq, D_BRANCH)
    else:
        ob, wkv1 = _rwkv(sh, gb, wkv0, shift0[:, None, :], lw["rwkv"], tiles["rwkv"], tiles["chunk"],
                         tiles["chunk"], True)
    t_last = t_valid - (seq - tiles["mix"])
    x1, pool1, conv1 = _mix(x, oa, ga, ob, uc, gc, ud, gd, pool0, conv0, lw["mix"], final_g, batch, seq,
                            tiles["mix"], t_last, pos0, final)
    return x1, kt, vt, sh, lf_t, wkv1, pool1, conv1


def kernel(x_prompt, x_sample, cache_k, cache_v, cache_logf, state_wkv, state_shift, state_pool, state_conv,
           page_table, norm_g, w_in, fox_bf, rw_mu, rw_w0, rw_wup, rw_a0, rw_aup, rw_kk, rw_ka, rw_rk,
           rw_gn_g, rw_gn_b, pool_w, pool_scale, conv_w, conv_b, ln_g, ln_b, pw_out, w_out, final_norm_g):
    bp, tp, _ = x_prompt.shape
    bs, ts, _ = x_sample.shape
    depth = w_in.shape[0]
    n_phys = cache_k.shape[0]
    past = page_table.shape[1] * PAGE
    weights = (norm_g, w_in, fox_bf, rw_mu, rw_w0, rw_wup, rw_a0, rw_aup, rw_kk, rw_ka, rw_rk, rw_gn_g, rw_gn_b,
               pool_w, pool_scale, conv_w, conv_b, ln_g, ln_b, pw_out, w_out)

    tile_p = min(256, tp)
    tiles_p = dict(inproj=min(512, tp), rwkv=min(512, tp), chunk=min(64, tile_p), mix=tile_p)
    tiles_s = dict(inproj=min(256, bs * SAMPLE_PAD), rwkv=min(64, bs * SAMPLE_PAD), chunk=SAMPLE_PAD, mix=SAMPLE_PAD)

    xp = x_prompt
    xs = jnp.pad(x_sample, ((0, 0), (0, SAMPLE_PAD - ts), (0, 0)))
    cache_kt = jnp.transpose(cache_k, (0, 1, 3, 4, 2))
    cache_vt = jnp.transpose(cache_v, (0, 1, 3, 4, 2))
    cache_lft = jnp.transpose(cache_logf, (0, 1, 3, 2))

    wkv_zero = jnp.zeros((bp, N_HEADS, HEAD_DIM, HEAD_DIM), F32)
    shift_zero = jnp.zeros((bp, SHIFT_W), F32)
    pool_zero = jnp.zeros((bp, POOL_BUF, D_BRANCH), F32)
    conv_zero = jnp.zeros((bp, CONV_BUF, D_BRANCH), F32)

    def sample_heads(t):
        return t.reshape(N_HEADS, HEAD_DIM, bs, SAMPLE_PAD).transpose(2, 0, 1, 3)

    def pad_keys(t):
        return jnp.pad(t, [(0, 0)] * (t.ndim - 1) + [(0, PAGE - SAMPLE_PAD)])

    fg = final_norm_g.reshape(1, D_MODEL)
    outs_p, outs_s = [], []
    for l in range(depth):
        lw = _layer_weights(l, *weights)

        attend_p = lambda q, kt, vt, lf_t: _fox_prompt(q, kt, vt, lf_t, min(512, tp))
        xp, kt_p, vt_p, sh_p, lf_p, wkv_p, pool_p, conv_p = _stream_layer(
            xp, lw, attend_p, wkv_zero, shift_zero, pool_zero, conv_zero, tiles_p, tp, 0, False, fg, l == depth - 1)
        outs_p.append((kt_p.reshape(bp, N_HEADS, HEAD_DIM, tp), vt_p.reshape(bp, N_HEADS, HEAD_DIM, tp),
                       lf_p[:, 0:N_HEADS], wkv_p, sh_p[:, tp - 1], pool_p, conv_p))

        def attend_s(q, kt, vt, lf_t, l=l):
            lf_new = lf_t[0, 0:N_HEADS].reshape(N_HEADS, bs, SAMPLE_PAD).transpose(1, 0, 2)
            return _fox_sample(q, pad_keys(sample_heads(kt)), pad_keys(sample_heads(vt)), pad_keys(lf_new),
                               cache_kt, cache_vt, cache_lft, page_table, l, ts)

        xs, kt_s, vt_s, sh_s, lf_s, wkv_s, pool_s, conv_s = _stream_layer(
            xs, lw, attend_s, state_wkv[:, l], state_shift[:, l], state_pool[:, l], state_conv[:, l],
            tiles_s, ts, past, True, fg, l == depth - 1)
        outs_s.append((sample_heads(kt_s).transpose(0, 3, 1, 2)[:, 0:ts],
                       sample_heads(vt_s).transpose(0, 3, 1, 2)[:, 0:ts],
                       lf_s[0, 0:N_HEADS].reshape(N_HEADS, bs, SAMPLE_PAD).transpose(1, 2, 0)[:, 0:ts],
                       wkv_s, sh_s[:, ts - 1], pool_s, conv_s))

    y_prompt = xp
    y_sample = xs[:, 0:ts]

    def stk(outs, i):
        return jnp.stack([o[i] for o in outs], axis=1)

    k_prompt = stk(outs_p, 0).transpose(0, 1, 4, 2, 3)
    v_prompt = stk(outs_p, 1).transpose(0, 1, 4, 2, 3)
    logf_prompt = stk(outs_p, 2).transpose(0, 1, 3, 2)
    return (y_prompt, y_sample, k_prompt, v_prompt, logf_prompt,
            *(stk(outs_p, i) for i in range(3, 7)),
            *(stk(outs_s, i) for i in range(7)))
```

```python
import functools

import jax
import jax.numpy as jnp
from jax import lax
from jax.experimental import pallas as pl
from jax.experimental.pallas import tpu as pltpu

F32 = jnp.float32
BF16 = jnp.bfloat16

D_MODEL = 1024
D_BRANCH = 256
HEAD_DIM = 64
N_HEADS = 4
LORA = 64
SHIFT_W = 3 * D_BRANCH + 2 * LORA
PAGE = 128
POOL_WINDOWS = (2, 4, 8, 16)
POOL_BUF = 15
CONV_WIDTH = 31
CONV_BUF = 30
RMS_EPS = 1e-6
LN_EPS = 1e-5
GN_EPS = 64e-5
ATTN_SCALE = HEAD_DIM ** -0.5
NEG_BIG = -1e30
SAMPLE_PAD = 8
PAGES_PER_STEP = 32
RWKV_PIECE = 64
RWKV_GROUP = 32
VMEM_LIMIT = 56 * 1024 * 1024

NN = ((1,), (0,))
NT = ((1,), (1,))
TN = ((0,), (0,))


def _dg(a, b, dims=NN):
    return lax.dot_general(a, b, (dims, ((), ())), preferred_element_type=F32)


def _split3(x):
    hi = x.astype(BF16)
    r1 = x - hi.astype(F32)
    mid = r1.astype(BF16)
    lo = (r1 - mid.astype(F32)).astype(BF16)
    return hi, mid, lo


def _dot_f32(a, b, dims=NN):
    ah, am, _ = _split3(a)
    bh, bm, _ = _split3(b)
    return _dg(ah, bh, dims) + (_dg(ah, bm, dims) + _dg(am, bh, dims))


def _dot_f32_stacked(a, b):
    ah, am, _ = _split3(a)
    bh, bm, _ = _split3(b)
    return _dg(jnp.concatenate([ah, am, ah], axis=1), jnp.concatenate([bh, bh, bm], axis=0))


def _dot_exact_rhs(a, b_bf16, dims=NN):
    ah, am, al = _split3(a)
    return _dg(ah, b_bf16, dims) + (_dg(am, b_bf16, dims) + _dg(al, b_bf16, dims))


def _dot_exact_lhs(a_bf16, b, dims=NN):
    bh, bm, bl = _split3(b)
    return _dg(a_bf16, bh, dims) + (_dg(a_bf16, bm, dims) + _dg(a_bf16, bl, dims))


def _group_sum(a, ones_bf16):
    hi = a.astype(BF16)
    mid = (a - hi.astype(F32)).astype(BF16)
    return _dg(hi, ones_bf16) + _dg(mid, ones_bf16)


def _sigmoid(x):
    return 1.0 / (1.0 + jnp.exp(-x))


def _silu(x):
    return x * _sigmoid(x)


def _softplus(x):
    return jnp.maximum(x, 0.0) + jnp.log1p(jnp.exp(-jnp.abs(x)))


def _iota(shape, dim):
    return lax.broadcasted_iota(jnp.int32, shape, dim)


def _params(*sem):
    return pltpu.CompilerParams(dimension_semantics=sem, vmem_limit_bytes=VMEM_LIMIT)


_IN_SEGS = (("q", 256), ("ga", 256), ("sh", SHIFT_W), ("gb", 256),
            ("uc", 256), ("gc", 256), ("ud", 512), ("gd", 256))
_IN_COLS = sum(s for _, s in _IN_SEGS)


def _inproj_kernel(x_ref, g_ref, w_ref, wkv_ref, wf_ref, bf_ref, *out_refs):
    x = x_ref[0]
    ms = jnp.mean(x * x, axis=-1, keepdims=True)
    h = ((x * lax.rsqrt(ms + RMS_EPS)) * g_ref[...]).astype(BF16)
    off = 0
    for (_, size), o_ref in zip(_IN_SEGS, out_refs[:-3]):
        o_ref[0] = jnp.dot(h, w_ref[:, off:off + size], preferred_element_type=F32)
        off += size
    kt_ref, vt_ref, lf_ref = out_refs[-3:]
    kt_ref[0] = _dg(wkv_ref[0:D_BRANCH, :], h, NT)
    vt_ref[0] = _dg(wkv_ref[D_BRANCH:2 * D_BRANCH, :], h, NT)
    f = _dg(wf_ref[...], h, NT) + bf_ref[...]
    lf_ref[0] = -_softplus(-f)


def _inproj(x, g, w, wkv_t, wf, bf, tm):
    batch, seq, _ = x.shape
    out_shape = [jax.ShapeDtypeStruct((batch, seq, s), F32) for _, s in _IN_SEGS]
    out_shape += [jax.ShapeDtypeStruct((batch, D_BRANCH, seq), F32)] * 2
    out_shape.append(jax.ShapeDtypeStruct((batch, 8, seq), F32))
    out_specs = [pl.BlockSpec((1, tm, s), lambda b, i: (b, i, 0)) for _, s in _IN_SEGS]
    out_specs += [pl.BlockSpec((1, D_BRANCH, tm), lambda b, i: (b, 0, i))] * 2
    out_specs.append(pl.BlockSpec((1, 8, tm), lambda b, i: (b, 0, i)))

    def full(shape):
        return pl.BlockSpec(shape, lambda b, i: (0,) * len(shape))

    return pl.pallas_call(
        _inproj_kernel,
        grid=(batch, seq // tm),
        in_specs=[pl.BlockSpec((1, tm, D_MODEL), lambda b, i: (b, i, 0)),
                  full((1, D_MODEL)), full((D_MODEL, _IN_COLS)), full((2 * D_BRANCH, D_MODEL)),
                  full((8, D_MODEL)), full((8, 1))],
        out_specs=out_specs,
        out_shape=out_shape,
        compiler_params=_params("parallel", "parallel"),
        name="inproj",
    )(x, g, w, wkv_t, wf, bf)


def _fox_prompt_kernel(q_ref, kt_ref, vt_ref, lf_ref, o_ref, c_ref, kb_ref, vb_ref, m_ref, acc_ref, *, tq, seq):
    i = pl.program_id(1)
    heads = [slice(h * HEAD_DIM, (h + 1) * HEAD_DIM) for h in range(N_HEADS)]

    @pl.when(i == 0)
    def _per_sequence():
        upper = (_iota((128, 128), 0) <= _iota((128, 128), 1)).astype(BF16)
        carry = jnp.zeros((8, 1), F32)
        for blk in range(seq // 128):
            cs = _dot_exact_rhs(lf_ref[0, :, blk * 128:(blk + 1) * 128], upper) + carry
            c_ref[:, blk * 128:(blk + 1) * 128] = cs
            carry = cs[:, 127:128]
        kb_ref[...] = kt_ref[0].astype(BF16)
        for h, hs in enumerate(heads):
            vb_ref[h, 0:HEAD_DIM, :] = vt_ref[0, hs, :].astype(BF16)
            vb_ref[h, HEAD_DIM:2 * HEAD_DIM, :] = jnp.ones((HEAD_DIM, seq), BF16)

    row0 = pl.multiple_of(i * tq, tq)
    c_tile = c_ref[:, pl.ds(row0, tq)]
    causal = _iota((tq, tq), 1) <= _iota((tq, tq), 0)

    q_all = q_ref[0] * ATTN_SCALE
    q_heads = [q_all[:, hs].astype(BF16) for hs in heads]
    m_ref[...] = jnp.full(m_ref.shape, NEG_BIG, F32)
    acc_ref[...] = jnp.zeros(acc_ref.shape, F32)

    def tile_update(col0, masked):
        s_all = []
        for h, hs in enumerate(heads):
            kh = kb_ref[hs, pl.ds(col0, tq)]
            s = _dg(q_heads[h], kh) + (c_tile[h:h + 1, 0:1] - c_ref[h:h + 1, pl.ds(col0, tq)])
            s_all.append(jnp.where(causal, s, NEG_BIG) if masked else s)
        m_old = [m_ref[h] for h in range(N_HEADS)]
        m_new = [jnp.maximum(mo, jnp.max(s, axis=-1, keepdims=True)) for mo, s in zip(m_old, s_all)]
        p_all = [jnp.exp(s - jnp.tile(mn, (1, tq // 128))).astype(BF16) for s, mn in zip(s_all, m_new)]
        pv = [_dg(p, vb_ref[h, :, pl.ds(col0, tq)], NT) for h, p in enumerate(p_all)]
        for h in range(N_HEADS):
            acc_ref[h] = jnp.exp(m_old[h] - m_new[h]) * acc_ref[h] + pv[h]
            m_ref[h] = m_new[h]

    def body(j, carry):
        tile_update(pl.multiple_of(j * tq, tq), False)
        return carry

    lax.fori_loop(0, i, body, 0)
    tile_update(row0, True)
    for h, hs in enumerate(heads):
        a = acc_ref[h]
        o_ref[0, :, hs] = a[:, 0:HEAD_DIM] / a[:, HEAD_DIM:2 * HEAD_DIM]


def _fox_prompt(q, kt, vt, lf_t, tq):
    batch, seq, _ = q.shape
    kern = functools.partial(_fox_prompt_kernel, tq=tq, seq=seq)
    return pl.pallas_call(
        kern,
        grid=(batch, seq // tq),
        in_specs=[pl.BlockSpec((1, tq, D_BRANCH), lambda b, i: (b, i, 0)),
                  pl.BlockSpec((1, D_BRANCH, seq), lambda b, i: (b, 0, 0)),
                  pl.BlockSpec((1, D_BRANCH, seq), lambda b, i: (b, 0, 0)),
                  pl.BlockSpec((1, 8, seq), lambda b, i: (b, 0, 0))],
        out_specs=pl.BlockSpec((1, tq, D_BRANCH), lambda b, i: (b, i, 0)),
        out_shape=jax.ShapeDtypeStruct((batch, seq, D_BRANCH), F32),
        scratch_shapes=[pltpu.VMEM((8, seq), F32),
                        pltpu.VMEM((D_BRANCH, seq), BF16), pltpu.VMEM((N_HEADS, 2 * HEAD_DIM, seq), BF16),
                        pltpu.VMEM((N_HEADS, tq, 128), F32), pltpu.VMEM((N_HEADS, tq, 2 * HEAD_DIM), F32)],
        compiler_params=_params("parallel", "arbitrary"),
        name="fox_prompt",
    )(q, kt, vt, lf_t)


def _fox_sample_kernel(pt_ref, q_ref, kn_ref, vn_ref, lfn_ref, ck_ref, cv_ref, clf_ref, o_ref,
                       kbuf, vbuf, lfbuf, suffix_ref, total_ref, s_ref, ksem, vsem, lfsem,
                       *, layer, n_pages, t_valid):
    b = pl.program_id(0)
    slot = b % 2
    rows = N_HEADS * SAMPLE_PAD
    group = min(PAGES_PER_STEP, n_pages)

    def page_copies(seq, slot_, p):
        page = pt_ref[seq, p]
        return (pltpu.make_async_copy(ck_ref.at[page, layer], kbuf.at[slot_, p], ksem.at[slot_]),
                pltpu.make_async_copy(cv_ref.at[page, layer], vbuf.at[slot_, p], vsem.at[slot_]),
                pltpu.make_async_copy(clf_ref.at[page, layer], lfbuf.at[slot_, p, 0:N_HEADS], lfsem.at[slot_]))

    def start_fetch(seq, slot_):
        def body(p, carry):
            for cp in page_copies(seq, slot_, p):
                cp.start()
            return carry
        lax.fori_loop(0, n_pages, body, 0)

    def wait_fetch(seq, slot_):
        def body(p, carry):
            for cp in page_copies(seq, slot_, p):
                cp.wait()
            return carry
        lax.fori_loop(0, n_pages, body, 0)

    @pl.when(b == 0)
    def _first():
        lfbuf[...] = jnp.zeros(lfbuf.shape, F32)
        start_fetch(0, 0)

    @pl.when(b + 1 < pl.num_programs(0))
    def _prefetch_next():
        start_fetch(b + 1, 1 - slot)

    wait_fetch(b, slot)

    q_all = q_ref[0] * ATTN_SCALE
    q_heads = [q_all[:, h * HEAD_DIM:(h + 1) * HEAD_DIM].astype(BF16) for h in range(N_HEADS)]

    key4 = _iota((N_HEADS, PAGE), 1)
    lfbuf[slot, n_pages, 0:N_HEADS, :] = jnp.where(key4 < t_valid, lfn_ref[0], 0.0)
    lf_all = lfbuf[slot].reshape((n_pages + 1) * 8, PAGE)
    later = (_iota((PAGE, PAGE), 0) > _iota((PAGE, PAGE), 1)).astype(BF16)
    suffix_ref[...] = _dot_exact_rhs(lf_all, later)
    total_ref[...] = _dot_exact_rhs(lf_all, jnp.ones((PAGE, PAGE), BF16))

    def scores(kt, bias8, col0, mask):
        for h in range(N_HEADS):
            s = _dg(q_heads[h], kt[h].astype(BF16)) + bias8[h:h + 1, :]
            if mask is not None:
                s = jnp.where(mask, s, NEG_BIG)
            s_ref[h * SAMPLE_PAD:(h + 1) * SAMPLE_PAD, pl.ds(col0, PAGE)] = s

    new0 = n_pages * 8
    key = _iota((SAMPLE_PAD, PAGE), 1)
    qry = _iota((SAMPLE_PAD, PAGE), 0)
    scores(kn_ref[0], suffix_ref[new0:new0 + 8, :], n_pages * PAGE,
           jnp.logical_and(key <= qry, key < t_valid))

    def score_group(g, running):
        lo = n_pages - (g + 1) * group
        bias = [None] * group
        for i in reversed(range(group)):
            r0 = pl.multiple_of((lo + i) * 8, 8)
            bias[i] = suffix_ref[pl.ds(r0, 8), :] + running
            running = running + total_ref[pl.ds(r0, 8), :]
        col0 = pl.multiple_of(lo * PAGE, group * PAGE)
        for h in range(N_HEADS):
            kt_wide = jnp.concatenate([kbuf[slot, lo + i, h].astype(BF16) for i in range(group)], axis=1)
            bias_h = jnp.concatenate([b8[h:h + 1, :] for b8 in bias], axis=1)
            s_ref[h * SAMPLE_PAD:(h + 1) * SAMPLE_PAD, pl.ds(col0, group * PAGE)] = _dg(q_heads[h], kt_wide) + bias_h
        return running

    lax.fori_loop(0, n_pages // group, score_group, total_ref[new0:new0 + 8, :])

    s = s_ref[...]
    m = jnp.max(s, axis=-1, keepdims=True)
    p_all = jnp.exp(s - m)
    denom = jnp.sum(p_all, axis=-1, keepdims=True)
    s_ref[...] = p_all

    def weighted(vt, col0):
        return jnp.concatenate(
            [_dg(s_ref[h * SAMPLE_PAD:(h + 1) * SAMPLE_PAD, pl.ds(col0, PAGE)].astype(BF16),
                 vt[h].astype(BF16), NT) for h in range(N_HEADS)], axis=0)

    def value_group(g, acc):
        lo = g * group
        col0 = pl.multiple_of(lo * PAGE, group * PAGE)
        parts = []
        for h in range(N_HEADS):
            vt_wide = jnp.concatenate([vbuf[slot, lo + i, h].astype(BF16) for i in range(group)], axis=1)
            p_wide = s_ref[h * SAMPLE_PAD:(h + 1) * SAMPLE_PAD, pl.ds(col0, group * PAGE)].astype(BF16)
            parts.append(_dg(p_wide, vt_wide, NT))
        return acc + jnp.concatenate(parts, axis=0)

    acc = lax.fori_loop(0, n_pages // group, value_group, weighted(vn_ref[0], n_pages * PAGE))
    out = acc / denom
    for h in range(N_HEADS):
        o_ref[0, :, h * HEAD_DIM:(h + 1) * HEAD_DIM] = out[h * SAMPLE_PAD:(h + 1) * SAMPLE_PAD]


def _fox_sample(q, kt_new, vt_new, lf_new, cache_kt, cache_vt, cache_lft, page_table, layer, t_valid):
    batch, n_pages = page_table.shape
    rows = N_HEADS * SAMPLE_PAD
    assert n_pages % min(PAGES_PER_STEP, n_pages) == 0

    def per_seq(shape):
        return pl.BlockSpec((1,) + shape, lambda b, pt: (b,) + (0,) * len(shape))

    in_specs = [per_seq((SAMPLE_PAD, D_BRANCH)), per_seq((N_HEADS, HEAD_DIM, PAGE)),
                per_seq((N_HEADS, HEAD_DIM, PAGE)), per_seq((N_HEADS, PAGE)),
                pl.BlockSpec(memory_space=pl.ANY), pl.BlockSpec(memory_space=pl.ANY),
                pl.BlockSpec(memory_space=pl.ANY)]
    kern = functools.partial(_fox_sample_kernel, layer=layer, n_pages=n_pages, t_valid=t_valid)
    return pl.pallas_call(
        kern,
        grid_spec=pltpu.PrefetchScalarGridSpec(
            num_scalar_prefetch=1,
            grid=(batch,),
            in_specs=in_specs,
            out_specs=per_seq((SAMPLE_PAD, D_BRANCH)),
            scratch_shapes=[pltpu.VMEM((2, n_pages, N_HEADS, HEAD_DIM, PAGE), F32),
                            pltpu.VMEM((2, n_pages, N_HEADS, HEAD_DIM, PAGE), F32),
                            pltpu.VMEM((2, n_pages + 1, 8, PAGE), F32),
                            pltpu.VMEM(((n_pages + 1) * 8, PAGE), F32),
                            pltpu.VMEM(((n_pages + 1) * 8, PAGE), F32),
                            pltpu.VMEM((rows, (n_pages + 1) * PAGE), F32),
                            pltpu.SemaphoreType.DMA((2,)), pltpu.SemaphoreType.DMA((2,)),
                            pltpu.SemaphoreType.DMA((2,))]),
        out_shape=jax.ShapeDtypeStruct((batch, SAMPLE_PAD, D_BRANCH), F32),
        compiler_params=_params("arbitrary"),
        name="fox_sample",
    )(page_table, q, kt_new, vt_new, lf_new, cache_kt, cache_vt, cache_lft)


def _unit_lower_inverses(ns, size, chunk):
    r = _iota((size, size), 0)
    c = _iota((size, size), 1)
    eye = (r == c).astype(F32)
    pair = (r // 2) == (c // 2)
    ms = [eye - jnp.where(pair, n.astype(F32), 0.0) for n in ns]
    blk = 2
    while blk < chunk:
        off = jnp.logical_and((r // (2 * blk)) == (c // (2 * blk)), (r // blk) != (c // blk))
        mbs = [m.astype(BF16) for m in ms]
        ts = [_dg(mb, n).astype(BF16) for mb, n in zip(mbs, ns)]
        ts = [_dg(t, mb) for t, mb in zip(ts, mbs)]
        ms = [m - jnp.where(off, t, 0.0) for m, t in zip(ms, ts)]
        blk *= 2
    return ms


def _rwkv_kernel(sh_ref, gb_ref, s0_ref, prev0_ref, mu_ref, w0_ref, a0_ref, lora_ref, kkw_ref, ka_ref,
                 rk_ref, gng_ref, gnb_ref, o_ref, s1_ref, ext_ref, st_ref, oacc_ref, *, tb, chunk, t_valid,
                 carried):
    n_blk = tb // chunk
    sh = sh_ref[0]
    if carried:
        i = pl.program_id(1)

        @pl.when(i == 0)
        def _init():
            st_ref[...] = s0_ref[0]
            ext_ref[0:8, :] = jnp.zeros((8, SHIFT_W), F32)
            ext_ref[7:8, :] = prev0_ref[0]

        @pl.when(i > 0)
        def _carry_shift():
            ext_ref[7:8, :] = ext_ref[tb + 7:tb + 8, :]

        ext_ref[8:8 + tb, :] = sh
        prev = ext_ref[7:7 + tb, :]
    else:
        ext_ref[0:8, :] = jnp.zeros((8, SHIFT_W), F32)
        ext_ref[8:8 + tb, :] = sh
        first = (_iota((tb, 1), 0) % chunk) == 0
        prev = jnp.where(first, prev0_ref[0], ext_ref[7:7 + tb, :])
    xm = sh + mu_ref[...] * (prev - sh)
    r = xm[:, 0:256]
    k = xm[:, 256:512]
    v = xm[:, 512:768]
    wa = xm[:, 768:896]
    wa = jnp.where(_iota((tb, 2 * LORA), 1) < LORA, jnp.tanh(wa), wa)
    lora = jnp.dot(wa.astype(BF16), lora_ref[...], preferred_element_type=F32)
    w_log = -_softplus(-(w0_ref[...] + lora[:, 0:256])) - 0.5
    log_decay = -jnp.exp(w_log)
    a = _sigmoid(a0_ref[...] + lora[:, 256:512])

    grp = ((_iota((D_BRANCH, D_BRANCH), 0) // HEAD_DIM) == (_iota((D_BRANCH, D_BRANCH), 1) // HEAD_DIM)).astype(BF16)
    kk = k * kkw_ref[...]
    kk = kk * lax.rsqrt(jnp.maximum(_group_sum(kk * kk, grp), 1e-12))
    kp = k * (1.0 + (a - 1.0) * ka_ref[...])
    beta = kk * a
    bonus = _group_sum(r * kp * rk_ref[...], grp) * v
    if t_valid < chunk:
        keep = ((_iota((tb, 1), 0) % chunk) < t_valid).astype(F32)
        log_decay = log_decay * keep
        kp = kp * keep
        beta = beta * keep

    lc_rows = min(tb, RWKV_PIECE)
    rr = _iota((lc_rows, lc_rows), 0)
    cc = _iota((lc_rows, lc_rows), 1)
    incl = jnp.logical_and((rr // chunk) == (cc // chunk), rr >= cc).astype(BF16)
    lc = jnp.concatenate(
        [_dot_exact_lhs(incl, log_decay[r0:r0 + lc_rows]) for r0 in range(0, tb, lc_rows)], axis=0)
    lc_tot = jnp.concatenate(
        [jnp.broadcast_to(lc[(c + 1) * chunk - 1:(c + 1) * chunk, :], (chunk, D_BRANCH)) for c in range(n_blk)],
        axis=0)
    to_end = jnp.exp(lc_tot - lc)
    g_inv = jnp.exp(-lc)
    g_end = jnp.exp(lc_tot)
    qc = kk * jnp.exp(lc - log_decay)
    rc = r * jnp.exp(lc)
    kc = kp * g_inv
    bc = beta * g_inv
    kd = kp * to_end
    bd = beta * to_end
    eye_k = (_iota((HEAD_DIM, HEAD_DIM), 0) == _iota((HEAD_DIM, HEAD_DIM), 1)).astype(F32)

    def b16(t):
        return t.astype(BF16)

    mat = min(tb, RWKV_PIECE)
    n_sub = tb // mat
    rg = _iota((2 * mat, 2 * mat), 0)
    rl = rg % mat
    cl = _iota((2 * mat, 2 * mat), 1) % mat
    g_mask = jnp.logical_and((rl // chunk) == (cl // chunk), rl + rg // mat > cl)
    pieces = [(sub, h) for sub in range(n_sub) for h in range(N_HEADS)]

    def piece(t, sub, h):
        return t[sub * mat:(sub + 1) * mat, h * HEAD_DIM:(h + 1) * HEAD_DIM]

    zero_b = jnp.zeros((mat, HEAD_DIM), BF16)
    r_f, v_f, v_b, kd_f, bd_f, w_f, y_f, o0, rp = ([] for _ in range(9))
    for g0 in range(0, len(pieces), RWKV_GROUP):
        grp_pieces = pieces[g0:g0 + RWKV_GROUP]
        r_g = [piece(rc, *ph) for ph in grp_pieces]
        v_g = [piece(v, *ph) for ph in grp_pieces]
        q_b = [b16(piece(qc, *ph)) for ph in grp_pieces]
        k_b = [b16(piece(kc, *ph)) for ph in grp_pieces]
        b_b = [b16(piece(bc, *ph)) for ph in grp_pieces]
        r_b = [b16(t) for t in r_g]
        vb_g = [b16(t) for t in v_g]
        g_all = [b16(jnp.where(g_mask, _dg(jnp.concatenate([q, rr_], axis=0), jnp.concatenate([bb, k], axis=0), NT),
                               0.0)) for q, rr_, bb, k in zip(q_b, r_b, b_b, k_b)]
        g_top = [g[0:mat] for g in g_all]
        g_bot = [g[mat:2 * mat] for g in g_all]
        inv = [b16(t) for t in _unit_lower_inverses([g[:, 0:mat] for g in g_top], mat, chunk)]
        w_g = [_dg(m_, q) for m_, q in zip(inv, q_b)]
        av_b = [b16(_dg(g, jnp.concatenate([zero_b, vv], axis=0))) for g, vv in zip(g_top, vb_g)]
        y_g = [_dg(m_, av) for m_, av in zip(inv, av_b)]
        o0 += [_dg(g, jnp.concatenate([b16(-yy), vv], axis=0)) for g, yy, vv in zip(g_bot, y_g, vb_g)]
        rp += [rr_ - _dg(g, jnp.concatenate([b16(ww), zero_b], axis=0)) for rr_, g, ww in zip(r_g, g_bot, w_g)]
        r_f += r_g
        v_f += v_g
        v_b += vb_g
        w_f += w_g
        y_f += y_g
        kd_f += [piece(kd, *ph) for ph in grp_pieces]
        bd_f += [piece(bd, *ph) for ph in grp_pieces]

    blocks_per_sub = mat // chunk
    def block_piece(c, h):
        sub, loc = divmod(c, blocks_per_sub)
        return sub * N_HEADS + h, slice(loc * chunk, (loc + 1) * chunk)

    p_mats, z_mats = {}, {}
    for c in range(n_blk):
        for h in range(N_HEADS):
            idx, ls = block_piece(c, h)
            bd_c = b16(bd_f[idx][ls])
            g_c = g_end[c * chunk:c * chunk + 1, h * HEAD_DIM:(h + 1) * HEAD_DIM]
            p_mats[c, h] = eye_k * g_c - _dg(b16(w_f[idx][ls]), bd_c, TN)
            z_mats[c, h] = _dg(v_b[idx][ls] if chunk % 16 == 0 else b16(v_f[idx][ls]), b16(kd_f[idx][ls]), TN) \
                - _dg(b16(y_f[idx][ls]), bd_c, TN)

    states = [st_ref[h] for h in range(N_HEADS)] if carried else None
    s_start = {}
    for c in range(n_blk):
        for h in range(N_HEADS):
            s_in = states[h] if carried else s0_ref[c, h]
            s_start[c, h] = s_in
            s_out = _dot_f32(s_in, p_mats[c, h]) + z_mats[c, h]
            if carried:
                states[h] = s_out
            else:
                s1_ref[c, h] = s_out
    if carried:
        for h in range(N_HEADS):
            st_ref[h] = states[h]

    for c in range(n_blk):
        for h in range(N_HEADS):
            idx, ls = block_piece(c, h)
            oacc_ref[c * chunk:(c + 1) * chunk, h * HEAD_DIM:(h + 1) * HEAD_DIM] = \
                o0[idx][ls] + _dg(b16(rp[idx][ls]), b16(s_start[c, h]), NT)

    o = oacc_ref[...]
    mean = _group_sum(o, grp) * (1.0 / HEAD_DIM)
    d = o - mean
    var = _group_sum(d * d, grp) * (1.0 / HEAD_DIM)
    o = d * lax.rsqrt(var + GN_EPS) * gng_ref[...] + gnb_ref[...] + bonus
    o_ref[0] = o * _silu(gb_ref[0])
    if carried:
        s1_ref[0] = st_ref[...]


def _rwkv(sh, gb, s0, prev0, w, tb, chunk, t_valid, carried):
    batch, seq, _ = sh.shape
    kern = functools.partial(_rwkv_kernel, tb=tb, chunk=chunk, t_valid=t_valid, carried=carried)

    def row(width):
        return pl.BlockSpec((1, width), lambda b, i: (0, 0))

    if carried:
        state_spec = pl.BlockSpec((1, N_HEADS, HEAD_DIM, HEAD_DIM), lambda b, i: (b, 0, 0, 0))
        prev_spec = pl.BlockSpec((1, 1, SHIFT_W), lambda b, i: (b, 0, 0))
    else:
        state_spec = pl.BlockSpec((tb // chunk, N_HEADS, HEAD_DIM, HEAD_DIM), lambda b, i: (i, 0, 0, 0))
        prev_spec = pl.BlockSpec((1, tb, SHIFT_W), lambda b, i: (b, i, 0))
    return pl.pallas_call(
        kern,
        grid=(batch, seq // tb),
        in_specs=[pl.BlockSpec((1, tb, SHIFT_W), lambda b, i: (b, i, 0)),
                  pl.BlockSpec((1, tb, D_BRANCH), lambda b, i: (b, i, 0)),
                  state_spec, prev_spec,
                  row(SHIFT_W), row(D_BRANCH), row(D_BRANCH),
                  pl.BlockSpec((2 * LORA, 2 * D_BRANCH), lambda b, i: (0, 0)),
                  row(D_BRANCH), row(D_BRANCH), row(D_BRANCH), row(D_BRANCH), row(D_BRANCH)],
        out_specs=[pl.BlockSpec((1, tb, D_BRANCH), lambda b, i: (b, i, 0)), state_spec],
        out_shape=[jax.ShapeDtypeStruct((batch, seq, D_BRANCH), F32),
                   jax.ShapeDtypeStruct(s0.shape, F32)],
        scratch_shapes=[pltpu.VMEM((tb + 8, SHIFT_W), F32),
                        pltpu.VMEM((N_HEADS, HEAD_DIM, HEAD_DIM), F32),
                        pltpu.VMEM((tb, D_BRANCH), F32)],
        compiler_params=_params("parallel", "arbitrary"),
        name="rwkv",
    )(sh, gb, s0, prev0, w["mu"], w["w0"], w["a0"], w["lora"], w["kk"], w["ka"],
      w["rk"], w["gn_g"], w["gn_b"])


def _mix_kernel(x_ref, oa_ref, ga_ref, ob_ref, uc_ref, gc_ref, ud_ref, gd_ref, pool0_ref, conv0_ref,
                poolw_ref, pscale_ref, convw_ref, convb_ref, lng_ref, lnb_ref, pw_ref, wout_ref, fg_ref,
                xo_ref, pool1_ref, conv1_ref, pext_ref, cext_ref, pshift_ref, cshift_ref,
                *, tm, t_last, pos0, final):
    i = pl.program_id(1)
    pad_p = POOL_BUF + 1
    pad_c = CONV_BUF + 2

    @pl.when(i == 0)
    def _init():
        pext_ref[0:pad_p, :] = jnp.zeros((pad_p, D_BRANCH), F32)
        cext_ref[0:pad_c, :] = jnp.zeros((pad_c, D_BRANCH), F32)
        pext_ref[1:pad_p, :] = pool0_ref[0]
        cext_ref[2:pad_c, :] = conv0_ref[0]

    @pl.when(i > 0)
    def _carry():
        pext_ref[1:pad_p, :] = pext_ref[tm + 1:tm + pad_p, :]
        cext_ref[2:pad_c, :] = cext_ref[tm + 2:tm + pad_c, :]

    uc = uc_ref[0]
    pext_ref[pad_p:pad_p + tm, :] = uc
    ud = ud_ref[0]
    cext_ref[pad_c:pad_c + tm, :] = ud[:, 0:D_BRANCH] * _sigmoid(ud[:, D_BRANCH:2 * D_BRANCH])

    done_p = set()

    def back(d):
        a, sub = divmod(pad_p - d, 8)
        if sub not in done_p:
            done_p.add(sub)
            pshift_ref[sub] = pext_ref[sub:sub + tm + 8, :]
        return pshift_ref[sub, 8 * a:8 * a + tm, :]

    win2 = uc + back(1)
    win4 = win2 + back(2) + back(3)
    win8 = win4
    for d in range(4, 8):
        win8 = win8 + back(d)
    win16 = win8
    for d in range(8, 16):
        win16 = win16 + back(d)
    group = _iota((tm, D_BRANCH), 1) // HEAD_DIM
    win = jnp.where(group == 0, win2, jnp.where(group == 1, win4, jnp.where(group == 2, win8, win16)))
    width = jnp.where(group == 0, 2, jnp.where(group == 1, 4, jnp.where(group == 2, 8, 16)))
    pos = pos0 + i * tm + _iota((tm, D_BRANCH), 0)
    cnt = jnp.minimum(pos + 1, width).astype(F32)
    pooled = win / cnt - uc
    o_c = jnp.dot(pooled.astype(BF16), poolw_ref[...], preferred_element_type=F32) * pscale_ref[...]

    y = jnp.zeros((tm, D_BRANCH), F32) + convb_ref[...]
    for sub in range(8):
        taps = range(sub, CONV_WIDTH, 8)
        span = tm + 8 * (len(taps) - 1)
        cshift_ref[sub, 0:span, :] = cext_ref[2 + sub:2 + sub + span, :]
        for a, jt in enumerate(taps):
            y = y + cshift_ref[sub, 8 * a:8 * a + tm, :] * convw_ref[jt:jt + 1, :]
    mean = jnp.mean(y, axis=-1, keepdims=True)
    yc = y - mean
    var = jnp.mean(yc * yc, axis=-1, keepdims=True)
    yn = yc * lax.rsqrt(var + LN_EPS) * lng_ref[...] + lnb_ref[...]
    o_d = jnp.dot(_silu(yn).astype(BF16), pw_ref[...], preferred_element_type=F32)

    acc = jnp.dot((oa_ref[0] * _silu(ga_ref[0])).astype(BF16), wout_ref[0:256, :], preferred_element_type=F32)
    acc = acc + jnp.dot(ob_ref[0].astype(BF16), wout_ref[256:512, :], preferred_element_type=F32)
    acc = acc + jnp.dot((o_c * _silu(gc_ref[0])).astype(BF16), wout_ref[512:768, :], preferred_element_type=F32)
    acc = acc + jnp.dot((o_d * _silu(gd_ref[0])).astype(BF16), wout_ref[768:1024, :], preferred_element_type=F32)
    x_new = x_ref[0] + acc
    if final:
        ms = jnp.mean(x_new * x_new, axis=-1, keepdims=True)
        x_new = (x_new * lax.rsqrt(ms + RMS_EPS)) * fg_ref[...]
    xo_ref[0] = x_new

    pool1_ref[0] = pext_ref[t_last + 1:t_last + pad_p, :]
    conv1_ref[0] = cext_ref[t_last + 2:t_last + pad_c, :]


def _mix(x, oa, ga, ob, uc, gc, ud, gd, pool0, conv0, w, final_g, batch, seq, tm, t_last, pos0, final):
    def tok(t, width):
        return t.reshape(batch, seq, width)

    def tok_spec(width):
        return pl.BlockSpec((1, tm, width), lambda b, i: (b, i, 0))

    def full(shape):
        return pl.BlockSpec(shape, lambda b, i: (0,) * len(shape))

    pool_spec = pl.BlockSpec((1, POOL_BUF, D_BRANCH), lambda b, i: (b, 0, 0))
    conv_spec = pl.BlockSpec((1, CONV_BUF, D_BRANCH), lambda b, i: (b, 0, 0))
    kern = functools.partial(_mix_kernel, tm=tm, t_last=t_last, pos0=pos0, final=final)
    return pl.pallas_call(
        kern,
        grid=(batch, seq // tm),
        in_specs=[tok_spec(D_MODEL), tok_spec(256), tok_spec(256), tok_spec(256), tok_spec(256), tok_spec(256),
                  tok_spec(512), tok_spec(256), pool_spec, conv_spec,
                  full((D_BRANCH, D_BRANCH)), full((1, D_BRANCH)), full((CONV_WIDTH, D_BRANCH)),
                  full((1, D_BRANCH)), full((1, D_BRANCH)), full((1, D_BRANCH)),
                  full((D_BRANCH, D_BRANCH)), full((D_MODEL, D_MODEL)), full((1, D_MODEL))],
        out_specs=[tok_spec(D_MODEL), pool_spec, conv_spec],
        out_shape=[jax.ShapeDtypeStruct((batch, seq, D_MODEL), F32),
                   jax.ShapeDtypeStruct((batch, POOL_BUF, D_BRANCH), F32),
                   jax.ShapeDtypeStruct((batch, CONV_BUF, D_BRANCH), F32)],
        scratch_shapes=[pltpu.VMEM((POOL_BUF + 1 + tm, D_BRANCH), F32),
                        pltpu.VMEM((CONV_BUF + 2 + tm, D_BRANCH), F32),
                        pltpu.VMEM((8, tm + 8, D_BRANCH), F32),
                        pltpu.VMEM((8, tm + 8 * ((CONV_WIDTH - 1) // 8), D_BRANCH), F32)],
        compiler_params=_params("parallel", "arbitrary"),
        name="mix",
    )(tok(x, D_MODEL), tok(oa, 256), tok(ga, 256), tok(ob, 256), tok(uc, 256), tok(gc, 256), tok(ud, 512),
      tok(gd, 256), pool0, conv0, w["pool_w"], w["pool_scale"], w["conv_w"], w["conv_b"], w["ln_g"], w["ln_b"],
      w["pw_out"], w["w_out"], final_g)


def _layer_weights(l, norm_g, w_in, fox_bf, rw_mu, rw_w0, rw_wup, rw_a0, rw_aup, rw_kk, rw_ka, rw_rk,
                   rw_gn_g, rw_gn_b, pool_w, pool_scale, conv_w, conv_b, ln_g, ln_b, pw_out, w_out):
    wi = w_in[l]
    o_f = 3 * D_BRANCH
    o_ga = o_f + N_HEADS
    main = jnp.concatenate([wi[:, 0:D_BRANCH], wi[:, o_ga:]], axis=1).astype(BF16)
    wkv_t = wi[:, D_BRANCH:o_f].T.astype(BF16)
    wf = jnp.zeros((8, D_MODEL), F32).at[0:N_HEADS].set(wi[:, o_f:o_ga].T).astype(BF16)
    bf = jnp.zeros((8, 1), F32).at[0:N_HEADS, 0].set(fox_bf[l])
    lora = jnp.zeros((2 * LORA, 2 * D_BRANCH), F32)
    lora = lora.at[0:LORA, 0:D_BRANCH].set(rw_wup[l]).at[LORA:, D_BRANCH:].set(rw_aup[l]).astype(BF16)
    pw_bd = jnp.zeros((D_BRANCH, D_BRANCH), F32)
    for g in range(len(POOL_WINDOWS)):
        gs = slice(g * HEAD_DIM, (g + 1) * HEAD_DIM)
        pw_bd = pw_bd.at[gs, gs].set(pool_w[l, g])

    def row(t):
        return t[l].reshape(1, -1)

    return dict(
        norm_g=row(norm_g), main=main, wkv_t=wkv_t, wf=wf, bf=bf,
        rwkv=dict(mu=row(rw_mu), w0=row(rw_w0), a0=row(rw_a0), lora=lora, kk=row(rw_kk), ka=row(rw_ka),
                  rk=row(rw_rk), gn_g=row(rw_gn_g), gn_b=row(rw_gn_b)),
        mix=dict(pool_w=pw_bd.astype(BF16), pool_scale=row(pool_scale), conv_w=conv_w[l], conv_b=row(conv_b),
                 ln_g=row(ln_g), ln_b=row(ln_b), pw_out=pw_out[l].astype(BF16), w_out=w_out[l].astype(BF16)))


def _stream_layer(x, lw, attend, wkv0, shift0, pool0, conv0, tiles, t_valid, pos0, flat_inproj, final_g, final):
    batch, seq, _ = x.shape
    x_in = x.reshape(1, batch * seq, D_MODEL) if flat_inproj else x
    *tok, kt, vt, lf_t = _inproj(x_in, lw["norm_g"], lw["main"], lw["wkv_t"], lw["wf"], lw["bf"], tiles["inproj"])
    q, ga, sh, gb, uc, gc, ud, gd = (t.reshape(batch, seq, t.shape[-1]) for t in tok)
    oa = attend(q, kt, vt, lf_t)
    if flat_inproj:
        prev0 = jnp.pad(shift0[:, None, :], ((0, 0), (0, seq - 1), (0, 0))).reshape(1, batch * seq, SHIFT_W)
        ob, wkv1 = _rwkv(sh.reshape(1, batch * seq, SHIFT_W), gb.reshape(1, batch * seq, D_BRANCH), wkv0, prev0,
                         lw["rwkv"], tiles["rwkv"], seq, t_valid, False)
        ob = ob.reshape(batch, seq, D_BRANCH)
    else:
        ob, wkv1 = _rwkv(sh, gb, wkv0, shift0[:, None, :], lw["rwkv"], tiles["rwkv"], tiles["chunk"],
                         tiles["chunk"], True)
    t_last = t_valid - (seq - tiles["mix"])
    x1, pool1, conv1 = _mix(x, oa, ga, ob, uc, gc, ud, gd, pool0, conv0, lw["mix"], final_g, batch, seq,
                            tiles["mix"], t_last, pos0, final)
    return x1, kt, vt, sh, lf_t, wkv1, pool1, conv1


def kernel(x_prompt, x_sample, cache_k, cache_v, cache_logf, state_wkv, state_shift, state_pool, state_conv,
           page_table, norm_g, w_in, fox_bf, rw_mu, rw_w0, rw_wup, rw_a0, rw_aup, rw_kk, rw_ka, rw_rk,
           rw_gn_g, rw_gn_b, pool_w, pool_scale, conv_w, conv_b, ln_g, ln_b, pw_out, w_out, final_norm_g):
    bp, tp, _ = x_prompt.shape
    bs, ts, _ = x_sample.shape
    depth = w_in.shape[0]
    n_phys = cache_k.shape[0]
    past = page_table.shape[1] * PAGE
    weights = (norm_g, w_in, fox_bf, rw_mu, rw_w0, rw_wup, rw_a0, rw_aup, rw_kk, rw_ka, rw_rk, rw_gn_g, rw_gn_b,
               pool_w, pool_scale, conv_w, conv_b, ln_g, ln_b, pw_out, w_out)

    tile_p = min(256, tp)
    tiles_p = dict(inproj=min(512, tp), rwkv=min(512, tp), chunk=min(64, tile_p), mix=min(512, tp))
    tiles_s = dict(inproj=min(256, bs * SAMPLE_PAD), rwkv=min(64, bs * SAMPLE_PAD), chunk=SAMPLE_PAD, mix=SAMPLE_PAD)

    xp = x_prompt
    xs = jnp.pad(x_sample, ((0, 0), (0, SAMPLE_PAD - ts), (0, 0)))
    cache_kt = jnp.transpose(cache_k, (0, 1, 3, 4, 2))
    cache_vt = jnp.transpose(cache_v, (0, 1, 3, 4, 2))
    cache_lft = jnp.transpose(cache_logf, (0, 1, 3, 2))

    wkv_zero = jnp.zeros((bp, N_HEADS, HEAD_DIM, HEAD_DIM), F32)
    shift_zero = jnp.zeros((bp, SHIFT_W), F32)
    pool_zero = jnp.zeros((bp, POOL_BUF, D_BRANCH), F32)
    conv_zero = jnp.zeros((bp, CONV_BUF, D_BRANCH), F32)

    def sample_heads(t):
        return t.reshape(N_HEADS, HEAD_DIM, bs, SAMPLE_PAD).transpose(2, 0, 1, 3)

    def pad_keys(t):
        return jnp.pad(t, [(0, 0)] * (t.ndim - 1) + [(0, PAGE - SAMPLE_PAD)])

    fg = final_norm_g.reshape(1, D_MODEL)
    outs_p, outs_s = [], []
    for l in range(depth):
        lw = _layer_weights(l, *weights)

        attend_p = lambda q, kt, vt, lf_t: _fox_prompt(q, kt, vt, lf_t, min(512, tp))
        xp, kt_p, vt_p, sh_p, lf_p, wkv_p, pool_p, conv_p = _stream_layer(
            xp, lw, attend_p, wkv_zero, shift_zero, pool_zero, conv_zero, tiles_p, tp, 0, False, fg, l == depth - 1)
        outs_p.append((kt_p.reshape(bp, N_HEADS, HEAD_DIM, tp), vt_p.reshape(bp, N_HEADS, HEAD_DIM, tp),
                       lf_p[:, 0:N_HEADS], wkv_p, sh_p[:, tp - 1], pool_p, conv_p))

        def attend_s(q, kt, vt, lf_t, l=l):
            lf_new = lf_t[0, 0:N_HEADS].reshape(N_HEADS, bs, SAMPLE_PAD).transpose(1, 0, 2)
            return _fox_sample(q, pad_keys(sample_heads(kt)), pad_keys(sample_heads(vt)), pad_keys(lf_new),
                               cache_kt, cache_vt, cache_lft, page_table, l, ts)

        xs, kt_s, vt_s, sh_s, lf_s, wkv_s, pool_s, conv_s = _stream_layer(
            xs, lw, attend_s, state_wkv[:, l], state_shift[:, l], state_pool[:, l], state_conv[:, l],
            tiles_s, ts, past, True, fg, l == depth - 1)
        outs_s.append((sample_heads(kt_s).transpose(0, 3, 1, 2)[:, 0:ts],
                       sample_heads(vt_s).transpose(0, 3, 1, 2)[:, 0:ts],
                       lf_s[0, 0:N_HEADS].reshape(N_HEADS, bs, SAMPLE_PAD).transpose(1, 2, 0)[:, 0:ts],
                       wkv_s, sh_s[:, ts - 1], pool_s, conv_s))

    y_prompt = xp
    y_sample = xs[:, 0:ts]

    def stk(outs, i):
        return jnp.stack([o[i] for o in outs], axis=1)

    k_prompt = stk(outs_p, 0).transpose(0, 1, 4, 2, 3)
    v_prompt = stk(outs_p, 1).transpose(0, 1, 4, 2, 3)
    logf_prompt = stk(outs_p, 2).transpose(0, 1, 3, 2)
    return (y_prompt, y_sample, k_prompt, v_prompt, logf_prompt,
            *(stk(outs_p, i) for i in range(3, 7)),
            *(stk(outs_s, i) for i in range(7)))
```

```python
import functools

import jax
import jax.numpy as jnp
from jax import lax
from jax.experimental import pallas as pl
from jax.experimental.pallas import tpu as pltpu

F32 = jnp.float32
BF16 = jnp.bfloat16

D_MODEL = 1024
D_BRANCH = 256
HEAD_DIM = 64
N_HEADS = 4
LORA = 64
SHIFT_W = 3 * D_BRANCH + 2 * LORA
PAGE = 128
POOL_WINDOWS = (2, 4, 8, 16)
POOL_BUF = 15
CONV_WIDTH = 31
CONV_BUF = 30
RMS_EPS = 1e-6
LN_EPS = 1e-5
GN_EPS = 64e-5
ATTN_SCALE = HEAD_DIM ** -0.5
NEG_BIG = -1e30
SAMPLE_PAD = 8
PAGES_PER_STEP = 32
RWKV_PIECE = 64
RWKV_GROUP = 32
VMEM_LIMIT = 56 * 1024 * 1024

NN = ((1,), (0,))
NT = ((1,), (1,))
TN = ((0,), (0,))


def _dg(a, b, dims=NN):
    return lax.dot_general(a, b, (dims, ((), ())), preferred_element_type=F32)


def _split3(x):
    hi = x.astype(BF16)
    r1 = x - hi.astype(F32)
    mid = r1.astype(BF16)
    lo = (r1 - mid.astype(F32)).astype(BF16)
    return hi, mid, lo


def _dot_f32(a, b, dims=NN):
    ah, am, _ = _split3(a)
    bh, bm, _ = _split3(b)
    return _dg(ah, bh, dims) + (_dg(ah, bm, dims) + _dg(am, bh, dims))


def _dot_f32_stacked(a, b):
    ah, am, _ = _split3(a)
    bh, bm, _ = _split3(b)
    return _dg(jnp.concatenate([ah, am, ah], axis=1), jnp.concatenate([bh, bh, bm], axis=0))


def _dot_exact_rhs(a, b_bf16, dims=NN):
    ah, am, al = _split3(a)
    return _dg(ah, b_bf16, dims) + (_dg(am, b_bf16, dims) + _dg(al, b_bf16, dims))


def _dot_exact_lhs(a_bf16, b, dims=NN):
    bh, bm, bl = _split3(b)
    return _dg(a_bf16, bh, dims) + (_dg(a_bf16, bm, dims) + _dg(a_bf16, bl, dims))


def _group_sum(a, ones_bf16):
    hi = a.astype(BF16)
    mid = (a - hi.astype(F32)).astype(BF16)
    return _dg(hi, ones_bf16) + _dg(mid, ones_bf16)


def _sigmoid(x):
    return 1.0 / (1.0 + jnp.exp(-x))


def _silu(x):
    return x * _sigmoid(x)


def _softplus(x):
    return jnp.maximum(x, 0.0) + jnp.log1p(jnp.exp(-jnp.abs(x)))


def _iota(shape, dim):
    return lax.broadcasted_iota(jnp.int32, shape, dim)


def _params(*sem):
    return pltpu.CompilerParams(dimension_semantics=sem, vmem_limit_bytes=VMEM_LIMIT)


_IN_SEGS = (("q", 256), ("ga", 256), ("sh", SHIFT_W), ("gb", 256),
            ("uc", 256), ("gc", 256), ("ud", 512), ("gd", 256))
_IN_COLS = sum(s for _, s in _IN_SEGS)


def _inproj_kernel(x_ref, g_ref, w_ref, wkv_ref, wf_ref, bf_ref, *out_refs):
    x = x_ref[0]
    ms = jnp.mean(x * x, axis=-1, keepdims=True)
    h = ((x * lax.rsqrt(ms + RMS_EPS)) * g_ref[...]).astype(BF16)
    off = 0
    for (_, size), o_ref in zip(_IN_SEGS, out_refs[:-3]):
        o_ref[0] = jnp.dot(h, w_ref[:, off:off + size], preferred_element_type=F32)
        off += size
    kt_ref, vt_ref, lf_ref = out_refs[-3:]
    kt_ref[0] = _dg(wkv_ref[0:D_BRANCH, :], h, NT)
    vt_ref[0] = _dg(wkv_ref[D_BRANCH:2 * D_BRANCH, :], h, NT)
    f = _dg(wf_ref[...], h, NT) + bf_ref[...]
    lf_ref[0] = -_softplus(-f)


def _inproj(x, g, w, wkv_t, wf, bf, tm):
    batch, seq, _ = x.shape
    out_shape = [jax.ShapeDtypeStruct((batch, seq, s), F32) for _, s in _IN_SEGS]
    out_shape += [jax.ShapeDtypeStruct((batch, D_BRANCH, seq), F32)] * 2
    out_shape.append(jax.ShapeDtypeStruct((batch, 8, seq), F32))
    out_specs = [pl.BlockSpec((1, tm, s), lambda b, i: (b, i, 0)) for _, s in _IN_SEGS]
    out_specs += [pl.BlockSpec((1, D_BRANCH, tm), lambda b, i: (b, 0, i))] * 2
    out_specs.append(pl.BlockSpec((1, 8, tm), lambda b, i: (b, 0, i)))

    def full(shape):
        return pl.BlockSpec(shape, lambda b, i: (0,) * len(shape))

    return pl.pallas_call(
        _inproj_kernel,
        grid=(batch, seq // tm),
        in_specs=[pl.BlockSpec((1, tm, D_MODEL), lambda b, i: (b, i, 0)),
                  full((1, D_MODEL)), full((D_MODEL, _IN_COLS)), full((2 * D_BRANCH, D_MODEL)),
                  full((8, D_MODEL)), full((8, 1))],
        out_specs=out_specs,
        out_shape=out_shape,
        compiler_params=_params("parallel", "parallel"),
        name="inproj",
    )(x, g, w, wkv_t, wf, bf)


def _fox_prompt_kernel(q_ref, kt_ref, vt_ref, lf_ref, o_ref, c_ref, kb_ref, vb_ref, m_ref, acc_ref, *, tq, seq):
    i = pl.program_id(1)
    heads = [slice(h * HEAD_DIM, (h + 1) * HEAD_DIM) for h in range(N_HEADS)]

    @pl.when(i == 0)
    def _per_sequence():
        upper = (_iota((128, 128), 0) <= _iota((128, 128), 1)).astype(BF16)
        carry = jnp.zeros((8, 1), F32)
        for blk in range(seq // 128):
            cs = _dot_exact_rhs(lf_ref[0, :, blk * 128:(blk + 1) * 128], upper) + carry
            c_ref[:, blk * 128:(blk + 1) * 128] = cs
            carry = cs[:, 127:128]
        kb_ref[...] = kt_ref[0].astype(BF16)
        for h, hs in enumerate(heads):
            vb_ref[h, 0:HEAD_DIM, :] = vt_ref[0, hs, :].astype(BF16)
            vb_ref[h, HEAD_DIM:2 * HEAD_DIM, :] = jnp.ones((HEAD_DIM, seq), BF16)

    row0 = pl.multiple_of(i * tq, tq)
    c_tile = c_ref[:, pl.ds(row0, tq)]
    causal = _iota((tq, tq), 1) <= _iota((tq, tq), 0)

    q_all = q_ref[0] * ATTN_SCALE
    q_heads = [q_all[:, hs].astype(BF16) for hs in heads]
    m_ref[...] = jnp.full(m_ref.shape, NEG_BIG, F32)
    acc_ref[...] = jnp.zeros(acc_ref.shape, F32)

    def tile_update(col0, masked):
        s_all = []
        for h, hs in enumerate(heads):
            kh = kb_ref[hs, pl.ds(col0, tq)]
            s = _dg(q_heads[h], kh) + (c_tile[h:h + 1, 0:1] - c_ref[h:h + 1, pl.ds(col0, tq)])
            s_all.append(jnp.where(causal, s, NEG_BIG) if masked else s)
        m_old = [m_ref[h] for h in range(N_HEADS)]
        m_new = [jnp.maximum(mo, jnp.max(s, axis=-1, keepdims=True)) for mo, s in zip(m_old, s_all)]
        p_all = [jnp.exp(s - jnp.tile(mn, (1, tq // 128))).astype(BF16) for s, mn in zip(s_all, m_new)]
        pv = [_dg(p, vb_ref[h, :, pl.ds(col0, tq)], NT) for h, p in enumerate(p_all)]
        for h in range(N_HEADS):
            acc_ref[h] = jnp.exp(m_old[h] - m_new[h]) * acc_ref[h] + pv[h]
            m_ref[h] = m_new[h]

    def body(j, carry):
        tile_update(pl.multiple_of(j * tq, tq), False)
        return carry

    lax.fori_loop(0, i, body, 0)
    tile_update(row0, True)
    for h, hs in enumerate(heads):
        a = acc_ref[h]
        o_ref[0, :, hs] = a[:, 0:HEAD_DIM] / a[:, HEAD_DIM:2 * HEAD_DIM]


def _fox_prompt(q, kt, vt, lf_t, tq):
    batch, seq, _ = q.shape
    kern = functools.partial(_fox_prompt_kernel, tq=tq, seq=seq)
    return pl.pallas_call(
        kern,
        grid=(batch, seq // tq),
        in_specs=[pl.BlockSpec((1, tq, D_BRANCH), lambda b, i: (b, i, 0)),
                  pl.BlockSpec((1, D_BRANCH, seq), lambda b, i: (b, 0, 0)),
                  pl.BlockSpec((1, D_BRANCH, seq), lambda b, i: (b, 0, 0)),
                  pl.BlockSpec((1, 8, seq), lambda b, i: (b, 0, 0))],
        out_specs=pl.BlockSpec((1, tq, D_BRANCH), lambda b, i: (b, i, 0)),
        out_shape=jax.ShapeDtypeStruct((batch, seq, D_BRANCH), F32),
        scratch_shapes=[pltpu.VMEM((8, seq), F32),
                        pltpu.VMEM((D_BRANCH, seq), BF16), pltpu.VMEM((N_HEADS, 2 * HEAD_DIM, seq), BF16),
                        pltpu.VMEM((N_HEADS, tq, 128), F32), pltpu.VMEM((N_HEADS, tq, 2 * HEAD_DIM), F32)],
        compiler_params=_params("parallel", "arbitrary"),
        name="fox_prompt",
    )(q, kt, vt, lf_t)


def _fox_sample_kernel(pt_ref, q_ref, kn_ref, vn_ref, lfn_ref, ck_ref, cv_ref, clf_ref, o_ref,
                       kbuf, vbuf, lfbuf, suffix_ref, total_ref, s_ref, ksem, vsem, lfsem,
                       *, layer, n_pages, t_valid):
    b = pl.program_id(0)
    slot = b % 2
    rows = N_HEADS * SAMPLE_PAD
    group = min(PAGES_PER_STEP, n_pages)

    def page_copies(seq, slot_, p):
        page = pt_ref[seq, p]
        return (pltpu.make_async_copy(ck_ref.at[page, layer], kbuf.at[slot_, p], ksem.at[slot_]),
                pltpu.make_async_copy(cv_ref.at[page, layer], vbuf.at[slot_, p], vsem.at[slot_]),
                pltpu.make_async_copy(clf_ref.at[page, layer], lfbuf.at[slot_, p, 0:N_HEADS], lfsem.at[slot_]))

    def start_fetch(seq, slot_):
        def body(p, carry):
            for cp in page_copies(seq, slot_, p):
                cp.start()
            return carry
        lax.fori_loop(0, n_pages, body, 0)

    def wait_fetch(seq, slot_):
        def body(p, carry):
            for cp in page_copies(seq, slot_, p):
                cp.wait()
            return carry
        lax.fori_loop(0, n_pages, body, 0)

    @pl.when(b == 0)
    def _first():
        lfbuf[...] = jnp.zeros(lfbuf.shape, F32)
        start_fetch(0, 0)

    @pl.when(b + 1 < pl.num_programs(0))
    def _prefetch_next():
        start_fetch(b + 1, 1 - slot)

    wait_fetch(b, slot)

    q_all = q_ref[0] * ATTN_SCALE
    q_heads = [q_all[:, h * HEAD_DIM:(h + 1) * HEAD_DIM].astype(BF16) for h in range(N_HEADS)]

    key4 = _iota((N_HEADS, PAGE), 1)
    lfbuf[slot, n_pages, 0:N_HEADS, :] = jnp.where(key4 < t_valid, lfn_ref[0], 0.0)
    lf_all = lfbuf[slot].reshape((n_pages + 1) * 8, PAGE)
    later = (_iota((PAGE, PAGE), 0) > _iota((PAGE, PAGE), 1)).astype(BF16)
    suffix_ref[...] = _dot_exact_rhs(lf_all, later)
    total_ref[...] = _dot_exact_rhs(lf_all, jnp.ones((PAGE, PAGE), BF16))

    def scores(kt, bias8, col0, mask):
        for h in range(N_HEADS):
            s = _dg(q_heads[h], kt[h].astype(BF16)) + bias8[h:h + 1, :]
            if mask is not None:
                s = jnp.where(mask, s, NEG_BIG)
            s_ref[h * SAMPLE_PAD:(h + 1) * SAMPLE_PAD, pl.ds(col0, PAGE)] = s

    new0 = n_pages * 8
    key = _iota((SAMPLE_PAD, PAGE), 1)
    qry = _iota((SAMPLE_PAD, PAGE), 0)
    scores(kn_ref[0], suffix_ref[new0:new0 + 8, :], n_pages * PAGE,
           jnp.logical_and(key <= qry, key < t_valid))

    def score_group(g, running):
        lo = n_pages - (g + 1) * group
        bias = [None] * group
        for i in reversed(range(group)):
            r0 = pl.multiple_of((lo + i) * 8, 8)
            bias[i] = suffix_ref[pl.ds(r0, 8), :] + running
            running = running + total_ref[pl.ds(r0, 8), :]
        col0 = pl.multiple_of(lo * PAGE, group * PAGE)
        for h in range(N_HEADS):
            kt_wide = jnp.concatenate([kbuf[slot, lo + i, h].astype(BF16) for i in range(group)], axis=1)
            bias_h = jnp.concatenate([b8[h:h + 1, :] for b8 in bias], axis=1)
            s_ref[h * SAMPLE_PAD:(h + 1) * SAMPLE_PAD, pl.ds(col0, group * PAGE)] = _dg(q_heads[h], kt_wide) + bias_h
        return running

    lax.fori_loop(0, n_pages // group, score_group, total_ref[new0:new0 + 8, :])

    s = s_ref[...]
    m = jnp.max(s, axis=-1, keepdims=True)
    p_all = jnp.exp(s - m)
    denom = jnp.sum(p_all, axis=-1, keepdims=True)
    s_ref[...] = p_all

    def weighted(vt, col0):
        return jnp.concatenate(
            [_dg(s_ref[h * SAMPLE_PAD:(h + 1) * SAMPLE_PAD, pl.ds(col0, PAGE)].astype(BF16),
                 vt[h].astype(BF16), NT) for h in range(N_HEADS)], axis=0)

    def value_group(g, acc):
        lo = g * group
        col0 = pl.multiple_of(lo * PAGE, group * PAGE)
        parts = []
        for h in range(N_HEADS):
            vt_wide = jnp.concatenate([vbuf[slot, lo + i, h].astype(BF16) for i in range(group)], axis=1)
            p_wide = s_ref[h * SAMPLE_PAD:(h + 1) * SAMPLE_PAD, pl.ds(col0, group * PAGE)].astype(BF16)
            parts.append(_dg(p_wide, vt_wide, NT))
        return acc + jnp.concatenate(parts, axis=0)

    acc = lax.fori_loop(0, n_pages // group, value_group, weighted(vn_ref[0], n_pages * PAGE))
    out = acc / denom
    for h in range(N_HEADS):
        o_ref[0, :, h * HEAD_DIM:(h + 1) * HEAD_DIM] = out[h * SAMPLE_PAD:(h + 1) * SAMPLE_PAD]


def _fox_sample(q, kt_new, vt_new, lf_new, cache_kt, cache_vt, cache_lft, page_table, layer, t_valid):
    batch, n_pages = page_table.shape
    rows = N_HEADS * SAMPLE_PAD
    assert n_pages % min(PAGES_PER_STEP, n_pages) == 0

    def per_seq(shape):
        return pl.BlockSpec((1,) + shape, lambda b, pt: (b,) + (0,) * len(shape))

    in_specs = [per_seq((SAMPLE_PAD, D_BRANCH)), per_seq((N_HEADS, HEAD_DIM, PAGE)),
                per_seq((N_HEADS, HEAD_DIM, PAGE)), per_seq((N_HEADS, PAGE)),
                pl.BlockSpec(memory_space=pl.ANY), pl.BlockSpec(memory_space=pl.ANY),
                pl.BlockSpec(memory_space=pl.ANY)]
    kern = functools.partial(_fox_sample_kernel, layer=layer, n_pages=n_pages, t_valid=t_valid)
    return pl.pallas_call(
        kern,
        grid_spec=pltpu.PrefetchScalarGridSpec(
            num_scalar_prefetch=1,
            grid=(batch,),
            in_specs=in_specs,
            out_specs=per_seq((SAMPLE_PAD, D_BRANCH)),
            scratch_shapes=[pltpu.VMEM((2, n_pages, N_HEADS, HEAD_DIM, PAGE), F32),
                            pltpu.VMEM((2, n_pages, N_HEADS, HEAD_DIM, PAGE), F32),
                            pltpu.VMEM((2, n_pages + 1, 8, PAGE), F32),
                            pltpu.VMEM(((n_pages + 1) * 8, PAGE), F32),
                            pltpu.VMEM(((n_pages + 1) * 8, PAGE), F32),
                            pltpu.VMEM((rows, (n_pages + 1) * PAGE), F32),
                            pltpu.SemaphoreType.DMA((2,)), pltpu.SemaphoreType.DMA((2,)),
                            pltpu.SemaphoreType.DMA((2,))]),
        out_shape=jax.ShapeDtypeStruct((batch, SAMPLE_PAD, D_BRANCH), F32),
        compiler_params=_params("arbitrary"),
        name="fox_sample",
    )(page_table, q, kt_new, vt_new, lf_new, cache_kt, cache_vt, cache_lft)


def _unit_lower_inverses(ns, size, chunk):
    r = _iota((size, size), 0)
    c = _iota((size, size), 1)
    eye = (r == c).astype(F32)
    pair = (r // 2) == (c // 2)
    ms = [eye - jnp.where(pair, n.astype(F32), 0.0) for n in ns]
    blk = 2
    while blk < chunk:
        off = jnp.logical_and((r // (2 * blk)) == (c // (2 * blk)), (r // blk) != (c // blk))
        mbs = [m.astype(BF16) for m in ms]
        ts = [_dg(mb, n).astype(BF16) for mb, n in zip(mbs, ns)]
        ts = [_dg(t, mb) for t, mb in zip(ts, mbs)]
        ms = [m - jnp.where(off, t, 0.0) for m, t in zip(ms, ts)]
        blk *= 2
    return ms


def _rwkv_kernel(sh_ref, gb_ref, s0_ref, prev0_ref, mu_ref, w0_ref, a0_ref, lora_ref, kkw_ref, ka_ref,
                 rk_ref, gng_ref, gnb_ref, o_ref, s1_ref, ext_ref, st_ref, oacc_ref, *, tb, chunk, t_valid,
                 carried):
    n_blk = tb // chunk
    sh = sh_ref[0]
    if carried:
        i = pl.program_id(1)

        @pl.when(i == 0)
        def _init():
            st_ref[...] = s0_ref[0]
            ext_ref[0:8, :] = jnp.zeros((8, SHIFT_W), F32)
            ext_ref[7:8, :] = prev0_ref[0]

        @pl.when(i > 0)
        def _carry_shift():
            ext_ref[7:8, :] = ext_ref[tb + 7:tb + 8, :]

        ext_ref[8:8 + tb, :] = sh
        prev = ext_ref[7:7 + tb, :]
    else:
        ext_ref[0:8, :] = jnp.zeros((8, SHIFT_W), F32)
        ext_ref[8:8 + tb, :] = sh
        first = (_iota((tb, 1), 0) % chunk) == 0
        prev = jnp.where(first, prev0_ref[0], ext_ref[7:7 + tb, :])
    xm = sh + mu_ref[...] * (prev - sh)
    r = xm[:, 0:256]
    k = xm[:, 256:512]
    v = xm[:, 512:768]
    wa = xm[:, 768:896]
    wa = jnp.where(_iota((tb, 2 * LORA), 1) < LORA, jnp.tanh(wa), wa)
    lora = jnp.dot(wa.astype(BF16), lora_ref[...], preferred_element_type=F32)
    w_log = -_softplus(-(w0_ref[...] + lora[:, 0:256])) - 0.5
    log_decay = -jnp.exp(w_log)
    a = _sigmoid(a0_ref[...] + lora[:, 256:512])

    grp = ((_iota((D_BRANCH, D_BRANCH), 0) // HEAD_DIM) == (_iota((D_BRANCH, D_BRANCH), 1) // HEAD_DIM)).astype(BF16)
    kk = k * kkw_ref[...]
    kk = kk * lax.rsqrt(jnp.maximum(_group_sum(kk * kk, grp), 1e-12))
    kp = k * (1.0 + (a - 1.0) * ka_ref[...])
    beta = kk * a
    bonus = _group_sum(r * kp * rk_ref[...], grp) * v
    if t_valid < chunk:
        keep = ((_iota((tb, 1), 0) % chunk) < t_valid).astype(F32)
        log_decay = log_decay * keep
        kp = kp * keep
        beta = beta * keep

    lc_rows = min(tb, RWKV_PIECE)
    rr = _iota((lc_rows, lc_rows), 0)
    cc = _iota((lc_rows, lc_rows), 1)
    incl = jnp.logical_and((rr // chunk) == (cc // chunk), rr >= cc).astype(BF16)
    lc = jnp.concatenate(
        [_dot_exact_lhs(incl, log_decay[r0:r0 + lc_rows]) for r0 in range(0, tb, lc_rows)], axis=0)
    lc_tot = jnp.concatenate(
        [jnp.broadcast_to(lc[(c + 1) * chunk - 1:(c + 1) * chunk, :], (chunk, D_BRANCH)) for c in range(n_blk)],
        axis=0)
    to_end = jnp.exp(lc_tot - lc)
    g_inv = jnp.exp(-lc)
    g_end = jnp.exp(lc_tot)
    qc = kk * jnp.exp(lc - log_decay)
    rc = r * jnp.exp(lc)
    kc = kp * g_inv
    bc = beta * g_inv
    kd = kp * to_end
    bd = beta * to_end
    eye_k = (_iota((HEAD_DIM, HEAD_DIM), 0) == _iota((HEAD_DIM, HEAD_DIM), 1)).astype(F32)

    def b16(t):
        return t.astype(BF16)

    mat = min(tb, RWKV_PIECE)
    n_sub = tb // mat
    rg = _iota((2 * mat, 2 * mat), 0)
    rl = rg % mat
    cl = _iota((2 * mat, 2 * mat), 1) % mat
    g_mask = jnp.logical_and((rl // chunk) == (cl // chunk), rl + rg // mat > cl)
    pieces = [(sub, h) for sub in range(n_sub) for h in range(N_HEADS)]

    def piece(t, sub, h):
        return t[sub * mat:(sub + 1) * mat, h * HEAD_DIM:(h + 1) * HEAD_DIM]

    zero_b = jnp.zeros((mat, HEAD_DIM), BF16)
    r_f, v_f, v_b, kd_f, bd_f, w_f, y_f, o0, rp = ([] for _ in range(9))
    for g0 in range(0, len(pieces), RWKV_GROUP):
        grp_pieces = pieces[g0:g0 + RWKV_GROUP]
        r_g = [piece(rc, *ph) for ph in grp_pieces]
        v_g = [piece(v, *ph) for ph in grp_pieces]
        q_b = [b16(piece(qc, *ph)) for ph in grp_pieces]
        k_b = [b16(piece(kc, *ph)) for ph in grp_pieces]
        b_b = [b16(piece(bc, *ph)) for ph in grp_pieces]
        r_b = [b16(t) for t in r_g]
        vb_g = [b16(t) for t in v_g]
        g_all = [b16(jnp.where(g_mask, _dg(jnp.concatenate([q, rr_], axis=0), jnp.concatenate([bb, k], axis=0), NT),
                               0.0)) for q, rr_, bb, k in zip(q_b, r_b, b_b, k_b)]
        g_top = [g[0:mat] for g in g_all]
        g_bot = [g[mat:2 * mat] for g in g_all]
        inv = [b16(t) for t in _unit_lower_inverses([g[:, 0:mat] for g in g_top], mat, chunk)]
        w_g = [_dg(m_, q) for m_, q in zip(inv, q_b)]
        av_b = [b16(_dg(g, jnp.concatenate([zero_b, vv], axis=0))) for g, vv in zip(g_top, vb_g)]
        y_g = [_dg(m_, av) for m_, av in zip(inv, av_b)]
        o0 += [_dg(g, jnp.concatenate([b16(-yy), vv], axis=0)) for g, yy, vv in zip(g_bot, y_g, vb_g)]
        rp += [rr_ - _dg(g, jnp.concatenate([b16(ww), zero_b], axis=0)) for rr_, g, ww in zip(r_g, g_bot, w_g)]
        r_f += r_g
        v_f += v_g
        v_b += vb_g
        w_f += w_g
        y_f += y_g
        kd_f += [piece(kd, *ph) for ph in grp_pieces]
        bd_f += [piece(bd, *ph) for ph in grp_pieces]

    blocks_per_sub = mat // chunk
    def block_piece(c, h):
        sub, loc = divmod(c, blocks_per_sub)
        return sub * N_HEADS + h, slice(loc * chunk, (loc + 1) * chunk)

    p_mats, z_mats = {}, {}
    for c in range(n_blk):
        for h in range(N_HEADS):
            idx, ls = block_piece(c, h)
            bd_c = b16(bd_f[idx][ls])
            g_c = g_end[c * chunk:c * chunk + 1, h * HEAD_DIM:(h + 1) * HEAD_DIM]
            p_mats[c, h] = eye_k * g_c - _dg(b16(w_f[idx][ls]), bd_c, TN)
            z_mats[c, h] = _dg(v_b[idx][ls] if chunk % 16 == 0 else b16(v_f[idx][ls]), b16(kd_f[idx][ls]), TN) \
                - _dg(b16(y_f[idx][ls]), bd_c, TN)

    states = [st_ref[h] for h in range(N_HEADS)] if carried else None
    s_start = {}
    for c in range(n_blk):
        for h in range(N_HEADS):
            s_in = states[h] if carried else s0_ref[c, h]
            s_start[c, h] = s_in
            s_out = _dot_f32(s_in, p_mats[c, h]) + z_mats[c, h]
            if carried:
                states[h] = s_out
            else:
                s1_ref[c, h] = s_out
    if carried:
        for h in range(N_HEADS):
            st_ref[h] = states[h]

    for c in range(n_blk):
        for h in range(N_HEADS):
            idx, ls = block_piece(c, h)
            oacc_ref[c * chunk:(c + 1) * chunk, h * HEAD_DIM:(h + 1) * HEAD_DIM] = \
                o0[idx][ls] + _dg(b16(rp[idx][ls]), b16(s_start[c, h]), NT)

    o = oacc_ref[...]
    mean = _group_sum(o, grp) * (1.0 / HEAD_DIM)
    d = o - mean
    var = _group_sum(d * d, grp) * (1.0 / HEAD_DIM)
    o = d * lax.rsqrt(var + GN_EPS) * gng_ref[...] + gnb_ref[...] + bonus
    o_ref[0] = o * _silu(gb_ref[0])
    if carried:
        s1_ref[0] = st_ref[...]


def _rwkv(sh, gb, s0, prev0, w, tb, chunk, t_valid, carried):
    batch, seq, _ = sh.shape
    kern = functools.partial(_rwkv_kernel, tb=tb, chunk=chunk, t_valid=t_valid, carried=carried)

    def row(width):
        return pl.BlockSpec((1, width), lambda b, i: (0, 0))

    if carried:
        state_spec = pl.BlockSpec((1, N_HEADS, HEAD_DIM, HEAD_DIM), lambda b, i: (b, 0, 0, 0))
        prev_spec = pl.BlockSpec((1, 1, SHIFT_W), lambda b, i: (b, 0, 0))
    else:
        state_spec = pl.BlockSpec((tb // chunk, N_HEADS, HEAD_DIM, HEAD_DIM), lambda b, i: (i, 0, 0, 0))
        prev_spec = pl.BlockSpec((1, tb, SHIFT_W), lambda b, i: (b, i, 0))
    return pl.pallas_call(
        kern,
        grid=(batch, seq // tb),
        in_specs=[pl.BlockSpec((1, tb, SHIFT_W), lambda b, i: (b, i, 0)),
                  pl.BlockSpec((1, tb, D_BRANCH), lambda b, i: (b, i, 0)),
                  state_spec, prev_spec,
                  row(SHIFT_W), row(D_BRANCH), row(D_BRANCH),
                  pl.BlockSpec((2 * LORA, 2 * D_BRANCH), lambda b, i: (0, 0)),
                  row(D_BRANCH), row(D_BRANCH), row(D_BRANCH), row(D_BRANCH), row(D_BRANCH)],
        out_specs=[pl.BlockSpec((1, tb, D_BRANCH), lambda b, i: (b, i, 0)), state_spec],
        out_shape=[jax.ShapeDtypeStruct((batch, seq, D_BRANCH), F32),
                   jax.ShapeDtypeStruct(s0.shape, F32)],
        scratch_shapes=[pltpu.VMEM((tb + 8, SHIFT_W), F32),
                        pltpu.VMEM((N_HEADS, HEAD_DIM, HEAD_DIM), F32),
                        pltpu.VMEM((tb, D_BRANCH), F32)],
        compiler_params=_params("parallel", "arbitrary"),
        name="rwkv",
    )(sh, gb, s0, prev0, w["mu"], w["w0"], w["a0"], w["lora"], w["kk"], w["ka"],
      w["rk"], w["gn_g"], w["gn_b"])


def _mix_kernel(x_ref, oa_ref, ga_ref, ob_ref, uc_ref, gc_ref, ud_ref, gd_ref, pool0_ref, conv0_ref,
                poolw_ref, pscale_ref, convw_ref, convb_ref, lng_ref, lnb_ref, pw_ref, wout_ref, fg_ref,
                xo_ref, pool1_ref, conv1_ref, pext_ref, cext_ref, pshift_ref, cshift_ref,
                *, tm, t_last, pos0, final):
    i = pl.program_id(1)
    pad_p = POOL_BUF + 1
    pad_c = CONV_BUF + 2

    @pl.when(i == 0)
    def _init():
        pext_ref[0:pad_p, :] = jnp.zeros((pad_p, D_BRANCH), F32)
        cext_ref[0:pad_c, :] = jnp.zeros((pad_c, D_BRANCH), F32)
        pext_ref[1:pad_p, :] = pool0_ref[0]
        cext_ref[2:pad_c, :] = conv0_ref[0]

    @pl.when(i > 0)
    def _carry():
        pext_ref[1:pad_p, :] = pext_ref[tm + 1:tm + pad_p, :]
        cext_ref[2:pad_c, :] = cext_ref[tm + 2:tm + pad_c, :]

    uc = uc_ref[0]
    pext_ref[pad_p:pad_p + tm, :] = uc
    ud = ud_ref[0]
    cext_ref[pad_c:pad_c + tm, :] = ud[:, 0:D_BRANCH] * _sigmoid(ud[:, D_BRANCH:2 * D_BRANCH])

    done_p = set()

    def back(d):
        a, sub = divmod(pad_p - d, 8)
        if sub not in done_p:
            done_p.add(sub)
            pshift_ref[sub] = pext_ref[sub:sub + tm + 8, :]
        return pshift_ref[sub, 8 * a:8 * a + tm, :]

    win2 = uc + back(1)
    win4 = win2 + back(2) + back(3)
    win8 = win4
    for d in range(4, 8):
        win8 = win8 + back(d)
    win16 = win8
    for d in range(8, 16):
        win16 = win16 + back(d)
    group = _iota((tm, D_BRANCH), 1) // HEAD_DIM
    win = jnp.where(group == 0, win2, jnp.where(group == 1, win4, jnp.where(group == 2, win8, win16)))
    width = jnp.where(group == 0, 2, jnp.where(group == 1, 4, jnp.where(group == 2, 8, 16)))
    pos = pos0 + i * tm + _iota((tm, D_BRANCH), 0)
    cnt = jnp.minimum(pos + 1, width).astype(F32)
    pooled = win / cnt - uc
    o_c = jnp.dot(pooled.astype(BF16), poolw_ref[...], preferred_element_type=F32) * pscale_ref[...]

    y = jnp.zeros((tm, D_BRANCH), F32) + convb_ref[...]
    for sub in range(8):
        taps = range(sub, CONV_WIDTH, 8)
        span = tm + 8 * (len(taps) - 1)
        cshift_ref[sub, 0:span, :] = cext_ref[2 + sub:2 + sub + span, :]
        for a, jt in enumerate(taps):
            y = y + cshift_ref[sub, 8 * a:8 * a + tm, :] * convw_ref[jt:jt + 1, :]
    mean = jnp.mean(y, axis=-1, keepdims=True)
    yc = y - mean
    var = jnp.mean(yc * yc, axis=-1, keepdims=True)
    yn = yc * lax.rsqrt(var + LN_EPS) * lng_ref[...] + lnb_ref[...]
    o_d = jnp.dot(_silu(yn).astype(BF16), pw_ref[...], preferred_element_type=F32)

    acc = jnp.dot((oa_ref[0] * _silu(ga_ref[0])).astype(BF16), wout_ref[0:256, :], preferred_element_type=F32)
    acc = acc + jnp.dot(ob_ref[0].astype(BF16), wout_ref[256:512, :], preferred_element_type=F32)
    acc = acc + jnp.dot((o_c * _silu(gc_ref[0])).astype(BF16), wout_ref[512:768, :], preferred_element_type=F32)
    acc = acc + jnp.dot((o_d * _silu(gd_ref[0])).astype(BF16), wout_ref[768:1024, :], preferred_element_type=F32)
    x_new = x_ref[0] + acc
    if final:
        ms = jnp.mean(x_new * x_new, axis=-1, keepdims=True)
        x_new = (x_new * lax.rsqrt(ms + RMS_EPS)) * fg_ref[...]
    xo_ref[0] = x_new

    pool1_ref[0] = pext_ref[t_last + 1:t_last + pad_p, :]
    conv1_ref[0] = cext_ref[t_last + 2:t_last + pad_c, :]


def _mix(x, oa, ga, ob, uc, gc, ud, gd, pool0, conv0, w, final_g, batch, seq, tm, t_last, pos0, final):
    def tok(t, width):
        return t.reshape(batch, seq, width)

    def tok_spec(width):
        return pl.BlockSpec((1, tm, width), lambda b, i: (b, i, 0))

    def full(shape):
        return pl.BlockSpec(shape, lambda b, i: (0,) * len(shape))

    pool_spec = pl.BlockSpec((1, POOL_BUF, D_BRANCH), lambda b, i: (b, 0, 0))
    conv_spec = pl.BlockSpec((1, CONV_BUF, D_BRANCH), lambda b, i: (b, 0, 0))
    kern = functools.partial(_mix_kernel, tm=tm, t_last=t_last, pos0=pos0, final=final)
    return pl.pallas_call(
        kern,
        grid=(batch, seq // tm),
        in_specs=[tok_spec(D_MODEL), tok_spec(256), tok_spec(256), tok_spec(256), tok_spec(256), tok_spec(256),
                  tok_spec(512), tok_spec(256), pool_spec, conv_spec,
                  full((D_BRANCH, D_BRANCH)), full((1, D_BRANCH)), full((CONV_WIDTH, D_BRANCH)),
                  full((1, D_BRANCH)), full((1, D_BRANCH)), full((1, D_BRANCH)),
                  full((D_BRANCH, D_BRANCH)), full((D_MODEL, D_MODEL)), full((1, D_MODEL))],
        out_specs=[tok_spec(D_MODEL), pool_spec, conv_spec],
        out_shape=[jax.ShapeDtypeStruct((batch, seq, D_MODEL), F32),
                   jax.ShapeDtypeStruct((batch, POOL_BUF, D_BRANCH), F32),
                   jax.ShapeDtypeStruct((batch, CONV_BUF, D_BRANCH), F32)],
        scratch_shapes=[pltpu.VMEM((POOL_BUF + 1 + tm, D_BRANCH), F32),
                        pltpu.VMEM((CONV_BUF + 2 + tm, D_BRANCH), F32),
                        pltpu.VMEM((8, tm + 8, D_BRANCH), F32),
                        pltpu.VMEM((8, tm + 8 * ((CONV_WIDTH - 1) // 8), D_BRANCH), F32)],
        compiler_params=_params("parallel", "arbitrary"),
        name="mix",
    )(tok(x, D_MODEL), tok(oa, 256), tok(ga, 256), tok(ob, 256), tok(uc, 256), tok(gc, 256), tok(ud, 512),
      tok(gd, 256), pool0, conv0, w["pool_w"], w["pool_scale"], w["conv_w"], w["conv_b"], w["ln_g"], w["ln_b"],
      w["pw_out"], w["w_out"], final_g)


def _layer_weights(l, norm_g, w_in, fox_bf, rw_mu, rw_w0, rw_wup, rw_a0, rw_aup, rw_kk, rw_ka, rw_rk,
                   rw_gn_g, rw_gn_b, pool_w, pool_scale, conv_w, conv_b, ln_g, ln_b, pw_out, w_out):
    wi = w_in[l]
    o_f = 3 * D_BRANCH
    o_ga = o_f + N_HEADS
    main = jnp.concatenate([wi[:, 0:D_BRANCH], wi[:, o_ga:]], axis=1).astype(BF16)
    wkv_t = wi[:, D_BRANCH:o_f].T.astype(BF16)
    wf = jnp.zeros((8, D_MODEL), F32).at[0:N_HEADS].set(wi[:, o_f:o_ga].T).astype(BF16)
    bf = jnp.zeros((8, 1), F32).at[0:N_HEADS, 0].set(fox_bf[l])
    lora = jnp.zeros((2 * LORA, 2 * D_BRANCH), F32)
    lora = lora.at[0:LORA, 0:D_BRANCH].set(rw_wup[l]).at[LORA:, D_BRANCH:].set(rw_aup[l]).astype(BF16)
    pw_bd = jnp.zeros((D_BRANCH, D_BRANCH), F32)
    for g in range(len(POOL_WINDOWS)):
        gs = slice(g * HEAD_DIM, (g + 1) * HEAD_DIM)
        pw_bd = pw_bd.at[gs, gs].set(pool_w[l, g])

    def row(t):
        return t[l].reshape(1, -1)

    return dict(
        norm_g=row(norm_g), main=main, wkv_t=wkv_t, wf=wf, bf=bf,
        rwkv=dict(mu=row(rw_mu), w0=row(rw_w0), a0=row(rw_a0), lora=lora, kk=row(rw_kk), ka=row(rw_ka),
                  rk=row(rw_rk), gn_g=row(rw_gn_g), gn_b=row(rw_gn_b)),
        mix=dict(pool_w=pw_bd.astype(BF16), pool_scale=row(pool_scale), conv_w=conv_w[l], conv_b=row(conv_b),
                 ln_g=row(ln_g), ln_b=row(ln_b), pw_out=pw_out[l].astype(BF16), w_out=w_out[l].astype(BF16)))


def _stream_layer(x, lw, attend, wkv0, shift0, pool0, conv0, tiles, t_valid, pos0, flat_inproj, final_g, final):
    batch, seq, _ = x.shape
    x_in = x.reshape(1, batch * seq, D_MODEL) if flat_inproj else x
    *tok, kt, vt, lf_t = _inproj(x_in, lw["norm_g"], lw["main"], lw["wkv_t"], lw["wf"], lw["bf"], tiles["inproj"])
    q, ga, sh, gb, uc, gc, ud, gd = (t.reshape(batch, seq, t.shape[-1]) for t in tok)
    oa = attend(q, kt, vt, lf_t)
    if flat_inproj:
        prev0 = jnp.pad(shift0[:, None, :], ((0, 0), (0, seq - 1), (0, 0))).reshape(1, batch * seq, SHIFT_W)
        ob, wkv1 = _rwkv(sh.reshape(1, batch * seq, SHIFT_W), gb.reshape(1, batch * seq, D_BRANCH), wkv0, prev0,
                         lw["rwkv"], tiles["rwkv"], seq, t_valid, False)
        ob = ob.reshape(batch, seq, D_BRANCH)
    else:
        ob, wkv1 = _rwkv(sh, gb, wkv0, shift0[:, None, :], lw["rwkv"], tiles["rwkv"], tiles["chunk"],
                         tiles["chunk"], True)
    t_last = t_valid - (seq - tiles["mix"])
    x1, pool1, conv1 = _mix(x, oa, ga, ob, uc, gc, ud, gd, pool0, conv0, lw["mix"], final_g, batch, seq,
                            tiles["mix"], t_last, pos0, final)
    return x1, kt, vt, sh, lf_t, wkv1, pool1, conv1


def kernel(x_prompt, x_sample, cache_k, cache_v, cache_logf, state_wkv, state_shift, state_pool, state_conv,
           page_table, norm_g, w_in, fox_bf, rw_mu, rw_w0, rw_wup, rw_a0, rw_aup, rw_kk, rw_ka, rw_rk,
           rw_gn_g, rw_gn_b, pool_w, pool_scale, conv_w, conv_b, ln_g, ln_b, pw_out, w_out, final_norm_g):
    bp, tp, _ = x_prompt.shape
    bs, ts, _ = x_sample.shape
    depth = w_in.shape[0]
    n_phys = cache_k.shape[0]
    past = page_table.shape[1] * PAGE
    weights = (norm_g, w_in, fox_bf, rw_mu, rw_w0, rw_wup, rw_a0, rw_aup, rw_kk, rw_ka, rw_rk, rw_gn_g, rw_gn_b,
               pool_w, pool_scale, conv_w, conv_b, ln_g, ln_b, pw_out, w_out)

    tile_p = min(256, tp)
    tiles_p = dict(inproj=min(512, tp), rwkv=min(512, tp), chunk=min(64, tile_p), mix=min(512, tp))
    tiles_s = dict(inproj=min(256, bs * SAMPLE_PAD), rwkv=min(256, bs * SAMPLE_PAD), chunk=SAMPLE_PAD, mix=SAMPLE_PAD)

    xp = x_prompt
    xs = jnp.pad(x_sample, ((0, 0), (0, SAMPLE_PAD - ts), (0, 0)))
    cache_kt = jnp.transpose(cache_k, (0, 1, 3, 4, 2))
    cache_vt = jnp.transpose(cache_v, (0, 1, 3, 4, 2))
    cache_lft = jnp.transpose(cache_logf, (0, 1, 3, 2))

    wkv_zero = jnp.zeros((bp, N_HEADS, HEAD_DIM, HEAD_DIM), F32)
    shift_zero = jnp.zeros((bp, SHIFT_W), F32)
    pool_zero = jnp.zeros((bp, POOL_BUF, D_BRANCH), F32)
    conv_zero = jnp.zeros((bp, CONV_BUF, D_BRANCH), F32)

    def sample_heads(t):
        return t.reshape(N_HEADS, HEAD_DIM, bs, SAMPLE_PAD).transpose(2, 0, 1, 3)

    def pad_keys(t):
        return jnp.pad(t, [(0, 0)] * (t.ndim - 1) + [(0, PAGE - SAMPLE_PAD)])

    fg = final_norm_g.reshape(1, D_MODEL)
    outs_p, outs_s = [], []
    for l in range(depth):
        lw = _layer_weights(l, *weights)

        attend_p = lambda q, kt, vt, lf_t: _fox_prompt(q, kt, vt, lf_t, min(512, tp))
        xp, kt_p, vt_p, sh_p, lf_p, wkv_p, pool_p, conv_p = _stream_layer(
            xp, lw, attend_p, wkv_zero, shift_zero, pool_zero, conv_zero, tiles_p, tp, 0, False, fg, l == depth - 1)
        outs_p.append((kt_p.reshape(bp, N_HEADS, HEAD_DIM, tp), vt_p.reshape(bp, N_HEADS, HEAD_DIM, tp),
                       lf_p[:, 0:N_HEADS], wkv_p, sh_p[:, tp - 1], pool_p, conv_p))

        def attend_s(q, kt, vt, lf_t, l=l):
            lf_new = lf_t[0, 0:N_HEADS].reshape(N_HEADS, bs, SAMPLE_PAD).transpose(1, 0, 2)
            return _fox_sample(q, pad_keys(sample_heads(kt)), pad_keys(sample_heads(vt)), pad_keys(lf_new),
                               cache_kt, cache_vt, cache_lft, page_table, l, ts)

        xs, kt_s, vt_s, sh_s, lf_s, wkv_s, pool_s, conv_s = _stream_layer(
            xs, lw, attend_s, state_wkv[:, l], state_shift[:, l], state_pool[:, l], state_conv[:, l],
            tiles_s, ts, past, True, fg, l == depth - 1)
        outs_s.append((sample_heads(kt_s).transpose(0, 3, 1, 2)[:, 0:ts],
                       sample_heads(vt_s).transpose(0, 3, 1, 2)[:, 0:ts],
                       lf_s[0, 0:N_HEADS].reshape(N_HEADS, bs, SAMPLE_PAD).transpose(1, 2, 0)[:, 0:ts],
                       wkv_s, sh_s[:, ts - 1], pool_s, conv_s))

    y_prompt = xp
    y_sample = xs[:, 0:ts]

    def stk(outs, i):
        return jnp.stack([o[i] for o in outs], axis=1)

    k_prompt = stk(outs_p, 0).transpose(0, 1, 4, 2, 3)
    v_prompt = stk(outs_p, 1).transpose(0, 1, 4, 2, 3)
    logf_prompt = stk(outs_p, 2).transpose(0, 1, 3, 2)
    return (y_prompt, y_sample, k_prompt, v_prompt, logf_prompt,
            *(stk(outs_p, i) for i in range(3, 7)),
            *(stk(outs_s, i) for i in range(7)))
```
